```python
import jax, jax.numpy as jnp
from jax import lax
import numpy as np

D_MODEL = 1024
BATCH = 4
SEQ = 8192
DEPTH = 2

N_EVEN = (DEPTH + 1) // 2
N_ODD = DEPTH // 2

DN_ALPHA = (2.0 * DEPTH) ** 0.25
DN_BETA = (8.0 * DEPTH) ** -0.25
LN_EPS = 1e-5

D_FF = 2816
FFN_RES = 0.5

CONV_DIM = D_MODEL // 2
CONV_WIDTH = 31
SB_HEADS = 8
SB_HEAD_DIM = 64
SB_DIM = SB_HEADS * SB_HEAD_DIM
Q_BLOCK = 128
IN0_DIM = 2 * CONV_DIM + 3 * SB_DIM
MIX0_DIM = CONV_DIM + SB_DIM

RW_HEADS = 8
RW_HEAD_DIM = 64
RW_DIM = RW_HEADS * RW_HEAD_DIM
DECAY_LORA = 64
ICLR_LORA = 64
GATE_LORA = 128
RW_IN_DIM = 3 * RW_DIM + DECAY_LORA + ICLR_LORA + GATE_LORA
RW_SPLITS = (RW_DIM, 2 * RW_DIM, 3 * RW_DIM, 3 * RW_DIM + DECAY_LORA, 3 * RW_DIM + DECAY_LORA + ICLR_LORA)
GN_EPS = RW_HEAD_DIM * 1e-5
POOL_WINDOWS = (2, 4, 8, 16)
POOL_GROUPS = 4
POOL_GROUP_DIM = 128
POOL_DIM = POOL_GROUPS * POOL_GROUP_DIM
IN1_DIM = RW_IN_DIM + POOL_DIM
MIX1_DIM = RW_DIM + POOL_DIM

kernel_name = "hybrid_conv_stickbreak_rwkv7_pool_deepnorm"


def layer_norm(x, g, b, eps=LN_EPS):
    xf = x.astype(jnp.float32)
    mu = jnp.mean(xf, axis=-1, keepdims=True)
    var = jnp.mean(jnp.square(xf - mu), axis=-1, keepdims=True)
    y = (xf - mu) * lax.rsqrt(var + eps)
    return (y * g.astype(jnp.float32) + b.astype(jnp.float32)).astype(x.dtype)


def swiglu(x, w_in, w_out):
    gate, up = jnp.split(x @ w_in, 2, axis=-1)
    return (jax.nn.silu(gate) * up) @ w_out


def conformer_conv(u, w_dw, b_dw, g, b):
    a, gate = jnp.split(u, 2, axis=-1)
    h = a * jax.nn.sigmoid(gate)
    h = lax.conv_general_dilated(h, w_dw[:, None, :], window_strides=(1,),
                                 padding=[(CONV_WIDTH - 1, 0)],
                                 dimension_numbers=('NWC', 'WIO', 'NWC'),
                                 feature_group_count=CONV_DIM) + b_dw
    return jax.nn.silu(layer_norm(h, g, b))


def stick_breaking_attention(q, k, v):
    seq = q.shape[2]
    scale = SB_HEAD_DIM ** -0.5
    outs = []
    for blk in range(seq // Q_BLOCK):
        start = blk * Q_BLOCK
        end = start + Q_BLOCK
        z = jnp.einsum('bhqd,bhkd->bhqk', q[:, :, start:end], k[:, :, :end],
                       preferred_element_type=jnp.float32) * scale
        mask = jnp.arange(end)[None, :] < jnp.arange(start, end)[:, None]
        log_keep = jnp.where(mask, jax.nn.log_sigmoid(-z), 0.0)
        log_tail = lax.cumsum(log_keep, axis=3, reverse=True) - log_keep
        att = jnp.where(mask, jnp.exp(jax.nn.log_sigmoid(z) + log_tail), 0.0)
        outs.append(jnp.einsum('bhqk,bhkd->bhqd', att.astype(v.dtype), v[:, :, :end]))
    return jnp.concatenate(outs, axis=2)


def even_mixer(x, w_in, w_dw, b_dw, conv_g, conv_b, w_out):
    bsz, seq, _ = x.shape
    h = x @ w_in
    y_conv = conformer_conv(h[..., :2 * CONV_DIM], w_dw, b_dw, conv_g, conv_b)
    qkv = h[..., 2 * CONV_DIM:].reshape(bsz, seq, 3, SB_HEADS, SB_HEAD_DIM)
    qkv = jnp.transpose(qkv, (2, 0, 3, 1, 4))
    y_att = stick_breaking_attention(qkv[0], qkv[1], qkv[2])
    y_att = jnp.transpose(y_att, (0, 2, 1, 3)).reshape(bsz, seq, SB_DIM)
    return jnp.concatenate([y_conv, y_att], axis=-1) @ w_out


def token_shift(p, mu):
    p_prev = jnp.pad(p, ((0, 0), (1, 0), (0, 0)))[:, :-1]
    return p + (p_prev - p) * mu


def rwkv7_recurrence(r, w, k, v, a, b):
    bsz, seq, nh, nd = r.shape

    def step(state, inp):
        r_t, w_t, k_t, v_t, a_t, b_t = inp
        sa = jnp.einsum('bhij,bhj->bhi', state, a_t)
        state = (state * w_t[:, :, None, :] + sa[..., :, None] * b_t[..., None, :]
                 + v_t[..., :, None] * k_t[..., None, :])
        return state, jnp.einsum('bhij,bhj->bhi', state, r_t)

    xs = (jnp.moveaxis(r, 1, 0), jnp.moveaxis(w, 1, 0), jnp.moveaxis(k, 1, 0),
          jnp.moveaxis(v, 1, 0), jnp.moveaxis(a, 1, 0), jnp.moveaxis(b, 1, 0))
    state0 = jnp.zeros((bsz, nh, nd, nd), jnp.float32)
    _, y = lax.scan(step, state0, xs)
    return jnp.moveaxis(y, 0, 1)


def rwkv7_time_mix(p, mu, w0, w2, a0, a2, g2, k_k, k_a, r_k, lnx_g, lnx_b):
    bsz, seq, _ = p.shape
    f32 = jnp.float32
    p = token_shift(p, mu)
    r, k, v, w_lr, a_lr, g_lr = jnp.split(p, RW_SPLITS, axis=-1)
    log_w = -jax.nn.softplus(-(w0 + jnp.tanh(w_lr) @ w2).astype(f32)) - 0.5
    decay = jnp.exp(-jnp.exp(log_w))
    iclr = jax.nn.sigmoid((a0 + a_lr @ a2).astype(f32))
    gate = (jax.nn.sigmoid(g_lr) @ g2).astype(f32)

    def heads(t):
        return t.astype(f32).reshape(bsz, seq, RW_HEADS, RW_HEAD_DIM)

    r_h, k_h, v_h, w_h, a_h = heads(r), heads(k), heads(v), heads(decay), heads(iclr)
    kk = k_h * k_k.astype(f32)
    kk = kk * lax.rsqrt(jnp.maximum(jnp.sum(kk * kk, axis=-1, keepdims=True), 1e-24))
    k_h = k_h * (1.0 + (a_h - 1.0) * k_a.astype(f32))
    y = rwkv7_recurrence(r_h, w_h, k_h, v_h, -kk, kk * a_h)
    y = layer_norm(y, lnx_g, lnx_b, GN_EPS)
    y = y + jnp.sum(r_h * k_h * r_k.astype(f32), axis=-1, keepdims=True) * v_h
    return (y.reshape(bsz, seq, RW_DIM) * gate).astype(p.dtype)


def multiscale_pool(u, w_pool, b_pool, scale):
    bsz, seq, _ = u.shape
    ug = u.astype(jnp.float32).reshape(bsz, seq, POOL_GROUPS, POOL_GROUP_DIM)
    csum = jnp.pad(jnp.cumsum(ug, axis=1), ((0, 0), (1, 0), (0, 0), (0, 0)))
    t = jnp.arange(seq)
    outs = []
    for gi, win in enumerate(POOL_WINDOWS):
        c = csum[:, :, gi]
        lo = jnp.pad(c, ((0, 0), (win - 1, 0), (0, 0)))[:, :seq]
        count = jnp.minimum(t + 1, win).astype(jnp.float32)[None, :, None]
        outs.append((c[:, 1:] - lo) / count)
    pooled = (jnp.stack(outs, axis=2) - ug).astype(u.dtype)
    y = jnp.einsum('bsgc,gcd->bsgd', pooled, w_pool) + b_pool
    return y.reshape(bsz, seq, POOL_DIM) * scale


def odd_mixer(x, w_in, mu, w0, w2, a0, a2, g2, k_k, k_a, r_k, lnx_g, lnx_b,
              w_pool, b_pool, pool_scale, w_out):
    h = x @ w_in
    y_rw = rwkv7_time_mix(h[..., :RW_IN_DIM], mu, w0, w2, a0, a2, g2, k_k, k_a, r_k,
                          lnx_g, lnx_b)
    y_pool = multiscale_pool(h[..., RW_IN_DIM:], w_pool, b_pool, pool_scale)
    return jnp.concatenate([y_rw, y_pool], axis=-1) @ w_out


def setup_inputs(seed: int = 0) -> dict:
    key = jax.random.key(seed)
    ks = jax.random.split(key, 27)
    f32 = jnp.float32

    def nrm(k, shape, std):
        return std * jax.random.normal(k, shape, f32)

    return {
        "x": jax.random.normal(ks[0], (BATCH, SEQ, D_MODEL), f32),
        "ffn_in": nrm(ks[1], (DEPTH, 2, D_MODEL, 2 * D_FF), D_MODEL ** -0.5),
        "ffn_out": nrm(ks[2], (DEPTH, 2, D_FF, D_MODEL), DN_BETA * D_FF ** -0.5),
        "ln_g": 1.0 + nrm(ks[3], (DEPTH, 3, D_MODEL), 0.02),
        "ln_b": nrm(ks[4], (DEPTH, 3, D_MODEL), 0.02),
        "e_w_in": nrm(ks[5], (N_EVEN, D_MODEL, IN0_DIM), D_MODEL ** -0.5),
        "e_w_dw": nrm(ks[6], (N_EVEN, CONV_WIDTH, CONV_DIM), CONV_WIDTH ** -0.5),
        "e_b_dw": nrm(ks[7], (N_EVEN, CONV_DIM), 0.02),
        "e_conv_g": 1.0 + nrm(ks[8], (N_EVEN, CONV_DIM), 0.02),
        "e_conv_b": nrm(ks[9], (N_EVEN, CONV_DIM), 0.02),
        "e_w_out": nrm(ks[10], (N_EVEN, MIX0_DIM, D_MODEL), DN_BETA * MIX0_DIM ** -0.5),
        "o_w_in": nrm(ks[11], (N_ODD, D_MODEL, IN1_DIM), D_MODEL ** -0.5),
        "o_mu": jax.random.uniform(ks[12], (N_ODD, RW_IN_DIM), f32, 0.0, 1.0),
        "o_w0": jax.random.uniform(ks[13], (N_ODD, RW_DIM), f32, -6.0, -1.0),
        "o_w2": nrm(ks[14], (N_ODD, DECAY_LORA, RW_DIM), 0.1 * DECAY_LORA ** -0.5),
        "o_a0": nrm(ks[15], (N_ODD, RW_DIM), 0.1),
        "o_a2": nrm(ks[16], (N_ODD, ICLR_LORA, RW_DIM), 0.1 * ICLR_LORA ** -0.5),
        "o_g2": nrm(ks[17], (N_ODD, GATE_LORA, RW_DIM), GATE_LORA ** -0.5),
        "o_k_k": 0.85 + nrm(ks[18], (N_ODD, RW_HEADS, RW_HEAD_DIM), 0.02),
        "o_k_a": 1.0 + nrm(ks[19], (N_ODD, RW_HEADS, RW_HEAD_DIM), 0.02),
        "o_r_k": nrm(ks[20], (N_ODD, RW_HEADS, RW_HEAD_DIM), 0.1),
        "o_lnx_g": 1.0 + nrm(ks[21], (N_ODD, RW_HEADS, RW_HEAD_DIM), 0.02),
        "o_lnx_b": nrm(ks[22], (N_ODD, RW_HEADS, RW_HEAD_DIM), 0.02),
        "o_w_pool": nrm(ks[23], (N_ODD, POOL_GROUPS, POOL_GROUP_DIM, POOL_GROUP_DIM), POOL_GROUP_DIM ** -0.5),
        "o_b_pool": nrm(ks[24], (N_ODD, POOL_GROUPS, POOL_GROUP_DIM), 0.02),
        "o_pool_scale": 0.5 + nrm(ks[25], (N_ODD, POOL_DIM), 0.1),
        "o_w_out": nrm(ks[26], (N_ODD, MIX1_DIM, D_MODEL), DN_BETA * MIX1_DIM ** -0.5),
    }


def reference(x, ffn_in, ffn_out, ln_g, ln_b,
              e_w_in, e_w_dw, e_b_dw, e_conv_g, e_conv_b, e_w_out,
              o_w_in, o_mu, o_w0, o_w2, o_a0, o_a2, o_g2, o_k_k, o_k_a, o_r_k,
              o_lnx_g, o_lnx_b, o_w_pool, o_b_pool, o_pool_scale, o_w_out):
    for layer in range(DEPTH):
        f = swiglu(x, ffn_in[layer, 0], ffn_out[layer, 0])
        x = layer_norm(DN_ALPHA * x + FFN_RES * f, ln_g[layer, 0], ln_b[layer, 0])
        if layer % 2 == 0:
            i = layer // 2
            m = even_mixer(x, e_w_in[i], e_w_dw[i], e_b_dw[i], e_conv_g[i], e_conv_b[i], e_w_out[i])
        else:
            i = layer // 2
            m = odd_mixer(x, o_w_in[i], o_mu[i], o_w0[i], o_w2[i], o_a0[i], o_a2[i], o_g2[i],
                          o_k_k[i], o_k_a[i], o_r_k[i], o_lnx_g[i], o_lnx_b[i],
                          o_w_pool[i], o_b_pool[i], o_pool_scale[i], o_w_out[i])
        x = layer_norm(DN_ALPHA * x + m, ln_g[layer, 1], ln_b[layer, 1])
        f = swiglu(x, ffn_in[layer, 1], ffn_out[layer, 1])
        x = layer_norm(DN_ALPHA * x + FFN_RES * f, ln_g[layer, 2], ln_b[layer, 2])
    return x
```

```python
import functools

import jax
import jax.numpy as jnp
from jax import lax
from jax.experimental import pallas as pl
from jax.experimental.pallas import tpu as pltpu

F32 = jnp.float32
BF16 = jnp.bfloat16

D_MODEL = 1024
DEPTH = 2
DN_ALPHA = (2.0 * DEPTH) ** 0.25
LN_EPS = 1e-5
D_FF = 2816
FFN_RES = 0.5
CONV_DIM = 512
CONV_WIDTH = 31
HEAD_DIM = 64
SB_DIM = 512
RW_DIM = 512
DECAY_LORA = 64
ICLR_LORA = 64
GATE_LORA = 128
RW_IN_DIM = 3 * RW_DIM + DECAY_LORA + ICLR_LORA + GATE_LORA
GN_EPS = HEAD_DIM * 1e-5
POOL_WINDOWS = (2, 4, 8, 16)
POOL_GROUP_DIM = 128
POOL_DIM = 512

LANES = 128
VMEM_LIMIT = 56 * 1024 * 1024

FFN_CHUNK = 256
RW_CHUNK = 64
INV_BLOCK = 16
ATT_BLOCK = 128
ATT_DEAD = -110.0


def _cparams(sem):
    return pltpu.CompilerParams(dimension_semantics=sem, vmem_limit_bytes=VMEM_LIMIT)


def _dot(a, b, dims="nn"):
    contract = {"nn": ((1,), (0,)), "nt": ((1,), (1,)), "tn": ((0,), (0,))}[dims]
    return lax.dot_general(a, b, (contract, ((), ())), preferred_element_type=F32)


def _split2(x):
    hi = x.astype(BF16)
    lo = (x - hi.astype(F32)).astype(BF16)
    return hi, lo


def _split3(x):
    hi = x.astype(BF16)
    r = x - hi.astype(F32)
    mid = r.astype(BF16)
    lo = (r - mid.astype(F32)).astype(BF16)
    return hi, mid, lo


def _dot3(a, b, dims="nn"):
    ah, al = _split2(a)
    bh, bl = _split2(b)
    return _dot(ah, bh, dims) + (_dot(ah, bl, dims) + _dot(al, bh, dims))


def _dot_exact_lhs(a_bf16, b):
    bh, bm, bl = _split3(b)
    return _dot(a_bf16, bh) + (_dot(a_bf16, bm) + _dot(a_bf16, bl))


def _dot_exact_rhs(a, b_bf16):
    ah, am, al = _split3(a)
    return _dot(ah, b_bf16) + (_dot(am, b_bf16) + _dot(al, b_bf16))


def _softplus(x):
    return jnp.maximum(x, 0.0) + jnp.log(1.0 + jnp.exp(-jnp.abs(x)))


def _layer_norm_rows(y, g, b, eps):
    mu = jnp.mean(y, axis=-1, keepdims=True)
    yc = y - mu
    var = jnp.mean(yc * yc, axis=-1, keepdims=True)
    return yc * lax.rsqrt(var + eps) * g + b


def _ffn_ln_kernel(x_ref, win_ref, wout_ref, g_ref, b_ref, o_ref, acc_ref):
    x = x_ref[...]
    xb = x.astype(BF16)
    for c in range(D_FF // FFN_CHUNK):
        lo = c * FFN_CHUNK
        gate = _dot(xb, win_ref[:, lo:lo + FFN_CHUNK])
        up = _dot(xb, win_ref[:, D_FF + lo:D_FF + lo + FFN_CHUNK])
        h = (gate * jax.nn.sigmoid(gate) * up).astype(BF16)
        part = _dot(h, wout_ref[lo:lo + FFN_CHUNK, :])
        if c == 0:
            acc_ref[...] = part
        else:
            acc_ref[...] += part
    y = DN_ALPHA * x + FFN_RES * acc_ref[...]
    o_ref[...] = _layer_norm_rows(y, g_ref[...], b_ref[...], LN_EPS)


def _ffn_ln(x2, w_in, w_out, g, b, tm=512):
    t, d = x2.shape
    assert t % tm == 0 and D_FF % FFN_CHUNK == 0
    const = lambda i: (0, 0)
    return pl.pallas_call(
        _ffn_ln_kernel,
        grid=(t // tm,),
        in_specs=[
            pl.BlockSpec((tm, d), lambda i: (i, 0)),
            pl.BlockSpec((d, 2 * D_FF), const, pipeline_mode=pl.Buffered(1)),
            pl.BlockSpec((D_FF, d), const, pipeline_mode=pl.Buffered(1)),
            pl.BlockSpec((1, d), const),
            pl.BlockSpec((1, d), const),
        ],
        out_specs=pl.BlockSpec((tm, d), lambda i: (i, 0)),
        out_shape=jax.ShapeDtypeStruct((t, d), F32),
        scratch_shapes=[pltpu.VMEM((tm, d), F32)],
        compiler_params=_cparams(("parallel",)),
        name="ffn_ln",
    )(x2, w_in.astype(BF16), w_out.astype(BF16), g.reshape(1, d), b.reshape(1, d))


def _proj_in_kernel(splits, x_ref, w_ref, *o_refs):
    xb = x_ref[...].astype(BF16)
    for (lo, hi), o_ref in zip(splits, o_refs):
        o_ref[...] = _dot(xb, w_ref[:, lo:hi]).astype(o_ref.dtype)


def _proj_in(x2, w, widths, dtypes, tm=512):
    t, d = x2.shape
    n = w.shape[1]
    assert t % tm == 0 and sum(widths) == n
    splits, lo = [], 0
    for wd in widths:
        splits.append((lo, lo + wd))
        lo += wd
    return pl.pallas_call(
        functools.partial(_proj_in_kernel, tuple(splits)),
        grid=(t // tm,),
        in_specs=[
            pl.BlockSpec((tm, d), lambda i: (i, 0)),
            pl.BlockSpec((d, n), lambda i: (0, 0), pipeline_mode=pl.Buffered(1)),
        ],
        out_specs=[pl.BlockSpec((tm, wd), lambda i: (i, 0)) for wd in widths],
        out_shape=[jax.ShapeDtypeStruct((t, wd), dt) for wd, dt in zip(widths, dtypes)],
        compiler_params=_cparams(("parallel",)),
        name="proj_in",
    )(x2, w.astype(BF16))


def _proj_out_ln_kernel(ya_ref, yb_ref, wa_ref, wb_ref, x_ref, g_ref, b_ref, o_ref):
    m = _dot(ya_ref[...], wa_ref[...]) + _dot(yb_ref[...], wb_ref[...])
    y = DN_ALPHA * x_ref[...] + m
    o_ref[...] = _layer_norm_rows(y, g_ref[...], b_ref[...], LN_EPS)


def _proj_out_ln(ya, yb, w_out, x2, g, b, tm=512):
    t, d = x2.shape
    ka, kb = ya.shape[1], yb.shape[1]
    assert t % tm == 0 and w_out.shape == (ka + kb, d)
    wb16 = w_out.astype(BF16)
    const = lambda i: (0, 0)
    row = lambda i: (i, 0)
    return pl.pallas_call(
        _proj_out_ln_kernel,
        grid=(t // tm,),
        in_specs=[
            pl.BlockSpec((tm, ka), row),
            pl.BlockSpec((tm, kb), row),
            pl.BlockSpec((ka, d), const),
            pl.BlockSpec((kb, d), const),
            pl.BlockSpec((tm, d), row),
            pl.BlockSpec((1, d), const),
            pl.BlockSpec((1, d), const),
        ],
        out_specs=pl.BlockSpec((tm, d), row),
        out_shape=jax.ShapeDtypeStruct((t, d), F32),
        compiler_params=_cparams(("parallel",)),
        name="proj_out_ln",
    )(ya, yb, wb16[:ka], wb16[ka:], x2, g.reshape(1, d), b.reshape(1, d))


CONV_HALO = 32


def _glu(u):
    return u[:, :CONV_DIM] * jax.nn.sigmoid(u[:, CONV_DIM:])


def _conv_kernel(u_ref, halo_ref, w_ref, bdw_ref, g_ref, b_ref, o_ref, hs_ref):
    ts = u_ref.shape[0]
    i = pl.program_id(1)
    halo = _glu(halo_ref[...])
    hs_ref[0:CONV_HALO, :] = jnp.where(i > 0, halo, 0.0)
    hs_ref[CONV_HALO:, :] = _glu(u_ref[...])
    base = CONV_HALO - (CONV_WIDTH - 1)
    acc = jnp.broadcast_to(bdw_ref[...], (ts, CONV_DIM))
    for k in range(CONV_WIDTH):
        acc = acc + w_ref[k:k + 1, :] * hs_ref[base + k:base + k + ts, :]
    y = _layer_norm_rows(acc, g_ref[...], b_ref[...], LN_EPS)
    o_ref[...] = (y * jax.nn.sigmoid(y)).astype(o_ref.dtype)


def _conformer_conv(u3, w_dw, b_dw, g, b, ts=256):
    bsz, seq, two_c = u3.shape
    assert seq % ts == 0 and ts % CONV_HALO == 0 and two_c == 2 * CONV_DIM
    r = ts // CONV_HALO
    const = lambda bi, i: (0, 0)
    return pl.pallas_call(
        _conv_kernel,
        grid=(bsz, seq // ts),
        in_specs=[
            pl.BlockSpec((None, ts, two_c), lambda bi, i: (bi, i, 0)),
            pl.BlockSpec((None, CONV_HALO, two_c), lambda bi, i: (bi, jnp.maximum(i * r - 1, 0), 0)),
            pl.BlockSpec((CONV_WIDTH, CONV_DIM), const),
            pl.BlockSpec((1, CONV_DIM), const),
            pl.BlockSpec((1, CONV_DIM), const),
            pl.BlockSpec((1, CONV_DIM), const),
        ],
        out_specs=pl.BlockSpec((None, ts, CONV_DIM), lambda bi, i: (bi, i, 0)),
        out_shape=jax.ShapeDtypeStruct((bsz, seq, CONV_DIM), BF16),
        scratch_shapes=[pltpu.VMEM((CONV_HALO + ts, CONV_DIM), F32)],
        compiler_params=_cparams(("parallel", "parallel")),
        name="conformer_conv",
    )(u3, u3, w_dw, b_dw.reshape(1, -1), g.reshape(1, -1), b.reshape(1, -1))


def _sb_block(qh, kb, vb, carry, u_aug, mask):
    tk = kb.shape[0]
    z = _dot(qh, kb, "nt")
    log_keep = -_softplus(z)
    if mask is not None:
        log_keep = jnp.where(mask, log_keep, 0.0)
    r = _dot_exact_rhs(log_keep, u_aug)
    incl = r[:, :tk]
    att = jnp.exp(z + carry + incl)
    if mask is not None:
        att = jnp.where(mask, att, 0.0)
    return _dot(att.astype(BF16), vb), carry + r[:, tk:]


def _sb_attn_kernel(q_ref, k_ref, v_ref, o_ref, acc_ref, carry_ref):
    tq = q_ref.shape[0]
    tk = tq
    i = pl.program_id(2)
    lane = lax.broadcasted_iota(jnp.int32, (tq, LANES), 1)
    head0 = lane < HEAD_DIM
    q = (q_ref[...].astype(F32) * (HEAD_DIM ** -0.5)).astype(BF16)
    zero = jnp.zeros_like(q)
    qs = (jnp.where(head0, q, zero), jnp.where(head0, zero, q))
    row = lax.broadcasted_iota(jnp.int32, (tk, 2 * tk), 0)
    col = lax.broadcasted_iota(jnp.int32, (tk, 2 * tk), 1)
    u_aug = jnp.where((row >= col) | (col >= tk), 1.0, 0.0).astype(BF16)
    qrow = lax.broadcasted_iota(jnp.int32, (tq, tk), 0)
    kcol = lax.broadcasted_iota(jnp.int32, (tq, tk), 1)
    diag_mask = kcol < qrow

    start = pl.multiple_of(i * tk, tk)
    kb = k_ref[pl.ds(start, tk), :]
    vb = v_ref[pl.ds(start, tk), :]
    live = None
    for h in range(2):
        pv, carry = _sb_block(qs[h], kb, vb, jnp.zeros((tq, LANES), F32), u_aug, diag_mask)
        acc_ref[h] = pv
        carry_ref[h] = carry
        m = jnp.max(carry)
        live = m if live is None else jnp.maximum(live, m)

    def cond(state):
        j, live = state
        return jnp.logical_and(j >= 0, live > ATT_DEAD)

    def body(state):
        j, _ = state
        start = pl.multiple_of(j * tk, tk)
        kb = k_ref[pl.ds(start, tk), :]
        vb = v_ref[pl.ds(start, tk), :]
        live = None
        for h in range(2):
            pv, carry = _sb_block(qs[h], kb, vb, carry_ref[h], u_aug, None)
            acc_ref[h] += pv
            carry_ref[h] = carry
            m = jnp.max(carry)
            live = m if live is None else jnp.maximum(live, m)
        return j - 1, live

    lax.while_loop(cond, body, (i - 1, live))
    o_ref[...] = jnp.where(head0, acc_ref[0], acc_ref[1]).astype(o_ref.dtype)


def _sb_attention(q3, k3, v3):
    bsz, seq, dim = q3.shape
    tq = ATT_BLOCK
    assert seq % tq == 0 and dim % LANES == 0
    qmap = lambda bi, hp, i: (bi, i, hp)
    kvmap = lambda bi, hp, i: (bi, 0, hp)
    return pl.pallas_call(
        _sb_attn_kernel,
        grid=(bsz, dim // LANES, seq // tq),
        in_specs=[
            pl.BlockSpec((None, tq, LANES), qmap),
            pl.BlockSpec((None, seq, LANES), kvmap),
            pl.BlockSpec((None, seq, LANES), kvmap),
        ],
        out_specs=pl.BlockSpec((None, tq, LANES), qmap),
        out_shape=jax.ShapeDtypeStruct((bsz, seq, dim), BF16),
        scratch_shapes=[pltpu.VMEM((2, tq, LANES), F32), pltpu.VMEM((2, tq, LANES), F32)],
        compiler_params=_cparams(("parallel", "parallel", "arbitrary")),
        name="sb_attention",
    )(q3, k3, v3)


RW_HALO = 8


def _head_sum_matrix():
    r = lax.broadcasted_iota(jnp.int32, (RW_DIM, RW_DIM), 0) // HEAD_DIM
    c = lax.broadcasted_iota(jnp.int32, (RW_DIM, RW_DIM), 1) // HEAD_DIM
    return jnp.where(r == c, 1.0, 0.0).astype(BF16)


def _rwkv_pre_kernel(p_ref, halo_ref, mu_ref, w0_ref, w2_ref, a0_ref, a2_ref, g2_ref, kk_ref, ka_ref,
                     r_ref, lw_ref, k_ref, v_ref, a_ref, b_ref, gate_ref):
    ts = p_ref.shape[0]
    i = pl.program_id(1)
    p = p_ref[...]
    prev_row = jnp.where(i > 0, halo_ref[RW_HALO - 1:RW_HALO, :], 0.0)
    rowid = lax.broadcasted_iota(jnp.int32, p.shape, 0)
    p_prev = jnp.where(rowid == 0, prev_row, pltpu.roll(p, 1, 0))
    ps = p + (p_prev - p) * mu_ref[...]
    r = ps[:, 0:RW_DIM]
    k = ps[:, RW_DIM:2 * RW_DIM]
    v = ps[:, 2 * RW_DIM:3 * RW_DIM]
    lora = ps[:, 3 * RW_DIM:3 * RW_DIM + LANES]
    g_lr = ps[:, 3 * RW_DIM + LANES:]
    log_w = -_softplus(-(w0_ref[...] + _dot3(jnp.tanh(lora), w2_ref[...]))) - 0.5
    lw_ref[...] = -jnp.exp(log_w)
    iclr = jax.nn.sigmoid(a0_ref[...] + _dot3(lora, a2_ref[...]))
    gate_ref[...] = _dot3(jax.nn.sigmoid(g_lr), g2_ref[...])
    kk = k * kk_ref[...]
    sumsq = _dot_exact_rhs(kk * kk, _head_sum_matrix())
    kk = kk * lax.rsqrt(jnp.maximum(sumsq, 1e-24))
    r_ref[...] = r
    k_ref[...] = k * (1.0 + (iclr - 1.0) * ka_ref[...])
    v_ref[...] = v
    a_ref[...] = -kk
    b_ref[...] = kk * iclr


def _rwkv_pre(p3, mu, w0, w2, a0, a2, g2, k_k, k_a, ts=256):
    bsz, seq, width = p3.shape
    assert seq % ts == 0 and ts % RW_HALO == 0 and width == RW_IN_DIM
    r = ts // RW_HALO
    zeros = jnp.zeros((DECAY_LORA, RW_DIM), F32)
    w2p = jnp.concatenate([w2, zeros], axis=0)
    a2p = jnp.concatenate([zeros, a2], axis=0)
    const = lambda bi, i: (0, 0)
    vec = pl.BlockSpec((1, RW_DIM), const)
    mat = pl.BlockSpec((LANES, RW_DIM), const)
    tile = pl.BlockSpec((None, ts, RW_DIM), lambda bi, i: (bi, i, 0))
    return pl.pallas_call(
        _rwkv_pre_kernel,
        grid=(bsz, seq // ts),
        in_specs=[
            pl.BlockSpec((None, ts, width), lambda bi, i: (bi, i, 0)),
            pl.BlockSpec((None, RW_HALO, width), lambda bi, i: (bi, jnp.maximum(i * r - 1, 0), 0)),
            pl.BlockSpec((1, width), const),
            vec, mat, vec, mat, mat, vec, vec,
        ],
        out_specs=[tile] * 7,
        out_shape=[jax.ShapeDtypeStruct((bsz, seq, RW_DIM), F32)] * 7,
        compiler_params=_cparams(("parallel", "parallel")),
        name="rwkv_pre",
    )(p3, p3, mu.reshape(1, -1), w0.reshape(1, -1), w2p, a0.reshape(1, -1), a2p, g2,
      k_k.reshape(1, -1), k_a.reshape(1, -1))


def _stack_heads(x, head0):
    zero = jnp.zeros_like(x)
    return jnp.concatenate([jnp.where(head0, x, zero), jnp.where(head0, zero, x)], axis=0)


def _fold_heads(x):
    n = x.shape[0] // 2
    return x[:n] + x[n:]


def _unit_lower_inverse(n_mat, same_block, eye):
    nd = jnp.where(same_block, n_mat, 0.0)
    rest = n_mat - nd
    dinv = eye + nd
    p = nd
    for _ in range(3):
        p = _dot3(p, p)
        dinv = dinv + _dot3(dinv, p)
    f = _dot3(dinv, rest)
    f2 = _dot3(f, f)
    g = eye + f
    g = g + _dot3(g, f2)
    return _dot3(g, dinv)


def _rwkv_rec_kernel(r_ref, lw_ref, k_ref, v_ref, a_ref, b_ref, y_ref, s_ref):
    c = RW_CHUNK

    @pl.when(pl.program_id(1) == 0)
    def _():
        s_ref[...] = jnp.zeros_like(s_ref)

    lw = lw_ref[...]
    tr = lax.broadcasted_iota(jnp.int32, (c, c), 0)
    tc = lax.broadcasted_iota(jnp.int32, (c, c), 1)
    cum = _dot_exact_lhs(jnp.where(tr >= tc, 1.0, 0.0).astype(BF16), lw)
    last = cum[c - 1:c, :]
    g_incl = jnp.exp(cum)
    g_excl = jnp.exp(cum - lw)
    g_inv = jnp.exp(-cum)
    g_rem = jnp.exp(last - cum)
    g_last = jnp.exp(last)
    a_t = a_ref[...] * g_excl
    r_t = r_ref[...] * g_incl
    b_t = b_ref[...] * g_inv
    k_t = k_ref[...] * g_inv
    b_h = b_ref[...] * g_rem
    k_h = k_ref[...] * g_rem
    v = v_ref[...]

    head0 = lax.broadcasted_iota(jnp.int32, (c, LANES), 1) < HEAD_DIM
    qr = lax.broadcasted_iota(jnp.int32, (4 * c, 4 * c), 0)
    qc = lax.broadcasted_iota(jnp.int32, (4 * c, 4 * c), 1)
    same_head = ((qr // c) % 2) == ((qc // c) % 2)
    t_idx, s_idx = qr % c, qc % c
    keep = same_head & ((s_idx < t_idx) | ((qr >= 2 * c) & (s_idx == t_idx)))
    pr = lax.broadcasted_iota(jnp.int32, (2 * c, 2 * c), 0)
    pc = lax.broadcasted_iota(jnp.int32, (2 * c, 2 * c), 1)
    same_block = (pr // INV_BLOCK) == (pc // INV_BLOCK)
    pair_diag = (pr // c) == (pc // c)
    eye = jnp.where(pr == pc, 1.0, 0.0)

    for m in range(RW_DIM // LANES):
        sl = slice(m * LANES, (m + 1) * LANES)
        at_p, rt_p, v_p = a_t[:, sl], r_t[:, sl], v[:, sl]
        lhs = jnp.concatenate([_stack_heads(at_p, head0), _stack_heads(rt_p, head0)], axis=0)
        rhs = jnp.concatenate([b_t[:, sl], b_t[:, sl], k_t[:, sl], k_t[:, sl]], axis=0)
        q = jnp.where(keep, _dot3(lhs, rhs, "nt"), 0.0)
        t_inv = _unit_lower_inverse(q[:2 * c, :2 * c], same_block, eye)
        a_ak = _fold_heads(q[:2 * c, 2 * c:])
        a_rb = _fold_heads(q[2 * c:, :2 * c])
        a_rk = _fold_heads(q[2 * c:, 2 * c:])
        s0 = s_ref[m]
        v_st = _stack_heads(v_p, head0)
        rhs_u = _dot3(at_p, s0, "nt") + _dot3(a_ak, v_st)
        u = _dot3(_fold_heads(t_inv), _stack_heads(rhs_u, head0))
        uv_st = jnp.concatenate([_stack_heads(u, head0), v_st], axis=0)
        y = _dot3(rt_p, s0, "nt") + _dot3(jnp.concatenate([a_rb, a_rk], axis=1), uv_st)
        y_ref[:, sl] = y
        outer = _dot3(jnp.concatenate([u, v_p], axis=0),
                      jnp.concatenate([b_h[:, sl], k_h[:, sl]], axis=0), "tn")
        s_ref[m] = s0 * g_last[:, sl] + jnp.where(pair_diag, outer, 0.0)


def _rwkv_recurrence(r, lw, k, v, a, b):
    bsz, seq, dim = r.shape
    c = RW_CHUNK
    assert seq % c == 0 and dim == RW_DIM
    tile = pl.BlockSpec((None, c, dim), lambda bi, i: (bi, i, 0))
    return pl.pallas_call(
        _rwkv_rec_kernel,
        grid=(bsz, seq // c),
        in_specs=[tile] * 6,
        out_specs=tile,
        out_shape=jax.ShapeDtypeStruct((bsz, seq, dim), F32),
        scratch_shapes=[pltpu.VMEM((dim // LANES, LANES, LANES), F32)],
        compiler_params=_cparams(("parallel", "arbitrary")),
        name="rwkv_recurrence",
    )(r, lw, k, v, a, b)


def _rwkv_post_kernel(y_ref, r_ref, k_ref, v_ref, gate_ref, rk_ref, g_ref, b_ref, o_ref):
    hsum = _head_sum_matrix()
    inv_n = 1.0 / HEAD_DIM
    y = y_ref[...]
    mu = _dot_exact_rhs(y, hsum) * inv_n
    yc = y - mu
    var = _dot_exact_rhs(yc * yc, hsum) * inv_n
    yn = yc * lax.rsqrt(var + GN_EPS) * g_ref[...] + b_ref[...]
    bonus = _dot_exact_rhs(r_ref[...] * k_ref[...] * rk_ref[...], hsum) * v_ref[...]
    o_ref[...] = ((yn + bonus) * gate_ref[...]).astype(o_ref.dtype)


def _rwkv_post(y, r, k, v, gate, r_k, lnx_g, lnx_b, tm=512):
    t, dim = y.shape
    assert t % tm == 0
    row = pl.BlockSpec((tm, dim), lambda i: (i, 0))
    vec = pl.BlockSpec((1, dim), lambda i: (0, 0))
    return pl.pallas_call(
        _rwkv_post_kernel,
        grid=(t // tm,),
        in_specs=[row] * 5 + [vec] * 3,
        out_specs=row,
        out_shape=jax.ShapeDtypeStruct((t, dim), BF16),
        compiler_params=_cparams(("parallel",)),
        name="rwkv_post",
    )(y, r, k, v, gate, r_k.reshape(1, -1), lnx_g.reshape(1, -1), lnx_b.reshape(1, -1))


POOL_HALO = 16


def _pool_kernel(u_ref, halo_ref, w_ref, b_ref, scale_ref, o_ref):
    ts = u_ref.shape[0]
    i = pl.program_id(1)
    pos = i * ts + lax.broadcasted_iota(jnp.int32, (ts, 1), 0)
    for gi, win in enumerate(POOL_WINDOWS):
        sl = slice(gi * POOL_GROUP_DIM, (gi + 1) * POOL_GROUP_DIM)
        u = u_ref[:, sl]
        halo = jnp.where(i > 0, halo_ref[:, sl], 0.0)
        ext = jnp.concatenate([halo, u], axis=0)
        shift = 1
        while shift < win:
            ext = ext + pltpu.roll(ext, shift, 0)
            shift *= 2
        count = jnp.minimum(pos + 1, win).astype(F32)
        pooled = ext[POOL_HALO:, :] / count - u
        y = _dot(pooled.astype(BF16), w_ref[gi]) + b_ref[:, sl]
        o_ref[:, sl] = (y * scale_ref[:, sl]).astype(o_ref.dtype)


def _multiscale_pool(u3, w_pool, b_pool, scale, ts=256):
    bsz, seq, dim = u3.shape
    assert seq % ts == 0 and ts % POOL_HALO == 0 and dim == POOL_DIM
    r = ts // POOL_HALO
    const = lambda bi, i: (0, 0)
    return pl.pallas_call(
        _pool_kernel,
        grid=(bsz, seq // ts),
        in_specs=[
            pl.BlockSpec((None, ts, dim), lambda bi, i: (bi, i, 0)),
            pl.BlockSpec((None, POOL_HALO, dim), lambda bi, i: (bi, jnp.maximum(i * r - 1, 0), 0)),
            pl.BlockSpec(w_pool.shape, lambda bi, i: (0, 0, 0)),
            pl.BlockSpec((1, dim), const),
            pl.BlockSpec((1, dim), const),
        ],
        out_specs=pl.BlockSpec((None, ts, dim), lambda bi, i: (bi, i, 0)),
        out_shape=jax.ShapeDtypeStruct((bsz, seq, dim), BF16),
        compiler_params=_cparams(("parallel", "parallel")),
        name="multiscale_pool",
    )(u3, u3, w_pool.astype(BF16), b_pool.reshape(1, -1), scale.reshape(1, -1))


def _even_mixer(x2, bsz, seq, w_in, w_dw, b_dw, conv_g, conv_b):
    u, q, k, v = _proj_in(x2, w_in, (2 * CONV_DIM, SB_DIM, SB_DIM, SB_DIM), (F32, BF16, BF16, BF16))
    y_conv = _conformer_conv(u.reshape(bsz, seq, -1), w_dw, b_dw, conv_g, conv_b)
    y_att = _sb_attention(q.reshape(bsz, seq, -1), k.reshape(bsz, seq, -1), v.reshape(bsz, seq, -1))
    return y_conv.reshape(bsz * seq, -1), y_att.reshape(bsz * seq, -1)


def _odd_mixer(x2, bsz, seq, w_in, mu, w0, w2, a0, a2, g2, k_k, k_a, r_k, lnx_g, lnx_b,
               w_pool, b_pool, pool_scale):
    p, u_pool = _proj_in(x2, w_in, (RW_IN_DIM, POOL_DIM), (F32, F32))
    r, lw, k, v, a, b, gate = _rwkv_pre(p.reshape(bsz, seq, -1), mu, w0, w2, a0, a2, g2, k_k, k_a)
    y = _rwkv_recurrence(r, lw, k, v, a, b)
    flat = lambda z: z.reshape(bsz * seq, -1)
    y_rw = _rwkv_post(flat(y), flat(r), flat(k), flat(v), flat(gate), r_k, lnx_g, lnx_b)
    y_pool = _multiscale_pool(u_pool.reshape(bsz, seq, -1), w_pool, b_pool, pool_scale)
    return y_rw, flat(y_pool)


def kernel(x, ffn_in, ffn_out, ln_g, ln_b, e_w_in, e_w_dw, e_b_dw, e_conv_g, e_conv_b, e_w_out, o_w_in, o_mu, o_w0, o_w2, o_a0, o_a2, o_g2, o_k_k, o_k_a, o_r_k, o_lnx_g, o_lnx_b, o_w_pool, o_b_pool, o_pool_scale, o_w_out):
    bsz, seq, d = x.shape
    x2 = x.reshape(bsz * seq, d)
    depth = ffn_in.shape[0]
    for layer in range(depth):
        x2 = _ffn_ln(x2, ffn_in[layer, 0], ffn_out[layer, 0], ln_g[layer, 0], ln_b[layer, 0])
        i = layer // 2
        if layer % 2 == 0:
            ya, yb = _even_mixer(x2, bsz, seq, e_w_in[i], e_w_dw[i], e_b_dw[i], e_conv_g[i], e_conv_b[i])
            w_out = e_w_out[i]
        else:
            ya, yb = _odd_mixer(x2, bsz, seq, o_w_in[i], o_mu[i], o_w0[i], o_w2[i], o_a0[i], o_a2[i],
                                o_g2[i], o_k_k[i], o_k_a[i], o_r_k[i], o_lnx_g[i], o_lnx_b[i],
                                o_w_pool[i], o_b_pool[i], o_pool_scale[i])
            w_out = o_w_out[i]
        x2 = _proj_out_ln(ya, yb, w_out, x2, ln_g[layer, 1], ln_b[layer, 1])
        x2 = _ffn_ln(x2, ffn_in[layer, 1], ffn_out[layer, 1], ln_g[layer, 2], ln_b[layer, 2])
    return x2.reshape(bsz, seq, d)
```

```python
import functools

import jax
import jax.numpy as jnp
from jax import lax
from jax.experimental import pallas as pl
from jax.experimental.pallas import tpu as pltpu

F32 = jnp.float32
BF16 = jnp.bfloat16

D_MODEL = 1024
DEPTH = 2
DN_ALPHA = (2.0 * DEPTH) ** 0.25
LN_EPS = 1e-5
D_FF = 2816
FFN_RES = 0.5
CONV_DIM = 512
CONV_WIDTH = 31
HEAD_DIM = 64
SB_DIM = 512
RW_DIM = 512
DECAY_LORA = 64
ICLR_LORA = 64
GATE_LORA = 128
RW_IN_DIM = 3 * RW_DIM + DECAY_LORA + ICLR_LORA + GATE_LORA
GN_EPS = HEAD_DIM * 1e-5
POOL_WINDOWS = (2, 4, 8, 16)
POOL_GROUP_DIM = 128
POOL_DIM = 512

LANES = 128
SUBLANES = 8
VMEM_LIMIT = 56 * 1024 * 1024

FFN_CHUNK = 256
RW_CHUNK = 64
RW_PASSES = 1
INV_BLOCK = 16
ATT_BLOCK = 128
ATT_DEAD = -110.0


def _cparams(sem):
    return pltpu.CompilerParams(dimension_semantics=sem, vmem_limit_bytes=VMEM_LIMIT)


def _dot(a, b, dims="nn"):
    contract = {"nn": ((1,), (0,)), "nt": ((1,), (1,)), "tn": ((0,), (0,))}[dims]
    return lax.dot_general(a, b, (contract, ((), ())), preferred_element_type=F32)


def _split2(x):
    hi = x.astype(BF16)
    lo = (x - hi.astype(F32)).astype(BF16)
    return hi, lo


def _split3(x):
    hi = x.astype(BF16)
    r = x - hi.astype(F32)
    mid = r.astype(BF16)
    lo = (r - mid.astype(F32)).astype(BF16)
    return hi, mid, lo


def _dot3(a, b, dims="nn"):
    ah, al = _split2(a)
    bh, bl = _split2(b)
    return _dot(ah, bh, dims) + (_dot(ah, bl, dims) + _dot(al, bh, dims))


def _dot_exact_lhs(a_bf16, b):
    bh, bm, bl = _split3(b)
    return _dot(a_bf16, bh) + (_dot(a_bf16, bm) + _dot(a_bf16, bl))


def _dot_exact_rhs(a, b_bf16):
    ah, am, al = _split3(a)
    return _dot(ah, b_bf16) + (_dot(am, b_bf16) + _dot(al, b_bf16))


def _parts(x, passes):
    return (x.astype(BF16),) if passes == 1 else _split2(x)


def _dot_parts(a, b, dims="nn"):
    out = _dot(a[0], b[0], dims)
    if len(a) > 1 and len(b) > 1:
        return out + (_dot(a[0], b[1], dims) + _dot(a[1], b[0], dims))
    if len(b) > 1:
        return out + _dot(a[0], b[1], dims)
    if len(a) > 1:
        return out + _dot(a[1], b[0], dims)
    return out


def _map_parts(fn, *parts):
    return tuple(fn(*p) for p in zip(*parts))


def _softplus(x):
    return jnp.maximum(x, 0.0) + jnp.log(1.0 + jnp.exp(-jnp.abs(x)))


def _layer_norm_rows(y, g, b, eps):
    mu = jnp.mean(y, axis=-1, keepdims=True)
    yc = y - mu
    var = jnp.mean(yc * yc, axis=-1, keepdims=True)
    return yc * lax.rsqrt(var + eps) * g + b


def _ffn_ln_kernel(x_ref, win_ref, wout_ref, g_ref, b_ref, o_ref, acc_ref):
    x = x_ref[...]
    xb = x.astype(BF16)
    for c in range(D_FF // FFN_CHUNK):
        lo = c * FFN_CHUNK
        gate = _dot(xb, win_ref[:, lo:lo + FFN_CHUNK])
        up = _dot(xb, win_ref[:, D_FF + lo:D_FF + lo + FFN_CHUNK])
        h = (gate * jax.nn.sigmoid(gate) * up).astype(BF16)
        part = _dot(h, wout_ref[lo:lo + FFN_CHUNK, :])
        if c == 0:
            acc_ref[...] = part
        else:
            acc_ref[...] += part
    y = DN_ALPHA * x + FFN_RES * acc_ref[...]
    o_ref[...] = _layer_norm_rows(y, g_ref[...], b_ref[...], LN_EPS)


def _ffn_ln(x2, w_in, w_out, g, b, tm=512):
    t, d = x2.shape
    assert t % tm == 0 and D_FF % FFN_CHUNK == 0
    const = lambda i: (0, 0)
    return pl.pallas_call(
        _ffn_ln_kernel,
        grid=(t // tm,),
        in_specs=[
            pl.BlockSpec((tm, d), lambda i: (i, 0)),
            pl.BlockSpec((d, 2 * D_FF), const, pipeline_mode=pl.Buffered(1)),
            pl.BlockSpec((D_FF, d), const, pipeline_mode=pl.Buffered(1)),
            pl.BlockSpec((1, d), const),
            pl.BlockSpec((1, d), const),
        ],
        out_specs=pl.BlockSpec((tm, d), lambda i: (i, 0)),
        out_shape=jax.ShapeDtypeStruct((t, d), F32),
        scratch_shapes=[pltpu.VMEM((tm, d), F32)],
        compiler_params=_cparams(("parallel",)),
        name="ffn_ln",
    )(x2, w_in.astype(BF16), w_out.astype(BF16), g.reshape(1, d), b.reshape(1, d))


def _proj_in_kernel(splits, x_ref, w_ref, *o_refs):
    xb = x_ref[...].astype(BF16)
    for (lo, hi), o_ref in zip(splits, o_refs):
        o_ref[...] = _dot(xb, w_ref[:, lo:hi]).astype(o_ref.dtype)


def _proj_in(x2, w, widths, dtypes, tm=512):
    t, d = x2.shape
    n = w.shape[1]
    assert t % tm == 0 and sum(widths) == n
    splits, lo = [], 0
    for wd in widths:
        splits.append((lo, lo + wd))
        lo += wd
    return pl.pallas_call(
        functools.partial(_proj_in_kernel, tuple(splits)),
        grid=(t // tm,),
        in_specs=[
            pl.BlockSpec((tm, d), lambda i: (i, 0)),
            pl.BlockSpec((d, n), lambda i: (0, 0), pipeline_mode=pl.Buffered(1)),
        ],
        out_specs=[pl.BlockSpec((tm, wd), lambda i: (i, 0)) for wd in widths],
        out_shape=[jax.ShapeDtypeStruct((t, wd), dt) for wd, dt in zip(widths, dtypes)],
        compiler_params=_cparams(("parallel",)),
        name="proj_in",
    )(x2, w.astype(BF16))


def _proj_out_ln_kernel(ya_ref, yb_ref, wa_ref, wb_ref, x_ref, g_ref, b_ref, o_ref):
    m = _dot(ya_ref[...], wa_ref[...]) + _dot(yb_ref[...], wb_ref[...])
    y = DN_ALPHA * x_ref[...] + m
    o_ref[...] = _layer_norm_rows(y, g_ref[...], b_ref[...], LN_EPS)


def _proj_out_ln(ya, yb, w_out, x2, g, b, tm=512):
    t, d = x2.shape
    ka, kb = ya.shape[1], yb.shape[1]
    assert t % tm == 0 and w_out.shape == (ka + kb, d)
    wb16 = w_out.astype(BF16)
    const = lambda i: (0, 0)
    row = lambda i: (i, 0)
    return pl.pallas_call(
        _proj_out_ln_kernel,
        grid=(t // tm,),
        in_specs=[
            pl.BlockSpec((tm, ka), row),
            pl.BlockSpec((tm, kb), row),
            pl.BlockSpec((ka, d), const),
            pl.BlockSpec((kb, d), const),
            pl.BlockSpec((tm, d), row),
            pl.BlockSpec((1, d), const),
            pl.BlockSpec((1, d), const),
        ],
        out_specs=pl.BlockSpec((tm, d), row),
        out_shape=jax.ShapeDtypeStruct((t, d), F32),
        compiler_params=_cparams(("parallel",)),
        name="proj_out_ln",
    )(ya, yb, wb16[:ka], wb16[ka:], x2, g.reshape(1, d), b.reshape(1, d))


CONV_HALO = 32
CONV_ROWS = 32


def _glu(u):
    return u[:, :CONV_DIM] * jax.nn.sigmoid(u[:, CONV_DIM:])


def _conv_kernel(u_ref, halo_ref, w_ref, bdw_ref, g_ref, b_ref, o_ref, hs_ref, acc_ref):
    ts = u_ref.shape[0]
    i = pl.program_id(1)
    halo = _glu(halo_ref[...])
    hs_ref[0, 0:CONV_HALO, :] = jnp.where(i > 0, halo, 0.0)
    hs_ref[0, CONV_HALO:, :] = _glu(u_ref[...])
    span = ts + CONV_HALO - SUBLANES
    for r in range(1, SUBLANES):
        hs_ref[r, 0:span, :] = hs_ref[0, r:r + span, :]
    base = CONV_HALO - (CONV_WIDTH - 1)
    bias = jnp.broadcast_to(bdw_ref[...], (CONV_ROWS, CONV_DIM))

    def row_block(rb, carry):
        row0 = pl.multiple_of(rb * CONV_ROWS, CONV_ROWS)
        acc = bias
        for k in range(CONV_WIDTH):
            off = base + k
            window = hs_ref[off % SUBLANES, pl.ds(row0 + (off // SUBLANES) * SUBLANES, CONV_ROWS), :]
            acc = acc + w_ref[k:k + 1, :] * window
        acc_ref[pl.ds(row0, CONV_ROWS), :] = acc
        return carry

    lax.fori_loop(0, ts // CONV_ROWS, row_block, 0)
    y = _layer_norm_rows(acc_ref[...], g_ref[...], b_ref[...], LN_EPS)
    o_ref[...] = (y * jax.nn.sigmoid(y)).astype(o_ref.dtype)


def _conformer_conv(u3, w_dw, b_dw, g, b, ts=256):
    bsz, seq, two_c = u3.shape
    assert seq % ts == 0 and ts % CONV_HALO == 0 and two_c == 2 * CONV_DIM
    r = ts // CONV_HALO
    const = lambda bi, i: (0, 0)
    return pl.pallas_call(
        _conv_kernel,
        grid=(bsz, seq // ts),
        in_specs=[
            pl.BlockSpec((None, ts, two_c), lambda bi, i: (bi, i, 0)),
            pl.BlockSpec((None, CONV_HALO, two_c), lambda bi, i: (bi, jnp.maximum(i * r - 1, 0), 0)),
            pl.BlockSpec((CONV_WIDTH, CONV_DIM), const),
            pl.BlockSpec((1, CONV_DIM), const),
            pl.BlockSpec((1, CONV_DIM), const),
            pl.BlockSpec((1, CONV_DIM), const),
        ],
        out_specs=pl.BlockSpec((None, ts, CONV_DIM), lambda bi, i: (bi, i, 0)),
        out_shape=jax.ShapeDtypeStruct((bsz, seq, CONV_DIM), BF16),
        scratch_shapes=[pltpu.VMEM((SUBLANES, CONV_HALO + ts, CONV_DIM), F32), pltpu.VMEM((ts, CONV_DIM), F32)],
        compiler_params=_cparams(("parallel", "parallel")),
        name="conformer_conv",
    )(u3, u3, w_dw, b_dw.reshape(1, -1), g.reshape(1, -1), b.reshape(1, -1))


def _sb_key_block(qs, kbs, vbs, carries, u_aug, mask):
    tk = kbs[0].shape[0]
    heads = range(len(qs))
    z = [_dot(qs[h], kbs[h // 2], "nt") for h in heads]
    log_keep = [-_softplus(z[h]) for h in heads]
    if mask is not None:
        log_keep = [jnp.where(mask, lk, 0.0) for lk in log_keep]
    r = [_dot_parts(_split2(lk), (u_aug,)) for lk in log_keep]
    att = [jnp.exp(z[h] + carries[h] + r[h][:, :tk]) for h in heads]
    if mask is not None:
        att = [jnp.where(mask, a, 0.0) for a in att]
    pv = [_dot(att[h].astype(BF16), vbs[h // 2]) for h in heads]
    return pv, [carries[h] + r[h][:, tk:] for h in heads]


def _sb_attn_kernel(q_ref, k_ref, v_ref, o_ref, acc_ref, carry_ref):
    tq = q_ref.shape[0]
    tk = tq
    n_pairs = q_ref.shape[1] // LANES
    i = pl.program_id(2)
    head0 = lax.broadcasted_iota(jnp.int32, (tq, LANES), 1) < HEAD_DIM
    qs = []
    for p in range(n_pairs):
        q = (q_ref[:, p * LANES:(p + 1) * LANES].astype(F32) * (HEAD_DIM ** -0.5)).astype(BF16)
        zero = jnp.zeros_like(q)
        qs += [jnp.where(head0, q, zero), jnp.where(head0, zero, q)]
    row = lax.broadcasted_iota(jnp.int32, (tk, 2 * tk), 0)
    col = lax.broadcasted_iota(jnp.int32, (tk, 2 * tk), 1)
    u_aug = jnp.where((row >= col) | (col >= tk), 1.0, 0.0).astype(BF16)
    qrow = lax.broadcasted_iota(jnp.int32, (tq, tk), 0)
    kcol = lax.broadcasted_iota(jnp.int32, (tq, tk), 1)

    def key_block(j, first):
        start = pl.multiple_of(j * tk, tk)
        kbs = [k_ref[pl.ds(start, tk), p * LANES:(p + 1) * LANES] for p in range(n_pairs)]
        vbs = [v_ref[pl.ds(start, tk), p * LANES:(p + 1) * LANES] for p in range(n_pairs)]
        if first:
            carries = [jnp.zeros((tq, LANES), F32)] * len(qs)
            pv, carries = _sb_key_block(qs, kbs, vbs, carries, u_aug, kcol < qrow)
        else:
            carries = [carry_ref[h] for h in range(len(qs))]
            pv, carries = _sb_key_block(qs, kbs, vbs, carries, u_aug, None)
        live = None
        for h in range(len(qs)):
            if first:
                acc_ref[h] = pv[h]
            else:
                acc_ref[h] += pv[h]
            carry_ref[h] = carries[h]
            live = carries[h] if live is None else jnp.maximum(live, carries[h])
        return jnp.max(live)

    live = key_block(i, True)

    def cond(state):
        j, live = state
        return jnp.logical_and(j >= 0, live > ATT_DEAD)

    def body(state):
        j, _ = state
        return j - 1, key_block(j, False)

    lax.while_loop(cond, body, (i - 1, live))
    for p in range(n_pairs):
        o_ref[:, p * LANES:(p + 1) * LANES] = jnp.where(
            head0, acc_ref[2 * p], acc_ref[2 * p + 1]).astype(o_ref.dtype)


def _sb_attention(q3, k3, v3, pairs=4):
    bsz, seq, dim = q3.shape
    tq = ATT_BLOCK
    width = pairs * LANES
    assert seq % tq == 0 and dim % width == 0
    qmap = lambda bi, hp, i: (bi, i, hp)
    kvmap = lambda bi, hp, i: (bi, 0, hp)
    return pl.pallas_call(
        _sb_attn_kernel,
        grid=(bsz, dim // width, seq // tq),
        in_specs=[
            pl.BlockSpec((None, tq, width), qmap),
            pl.BlockSpec((None, seq, width), kvmap),
            pl.BlockSpec((None, seq, width), kvmap),
        ],
        out_specs=pl.BlockSpec((None, tq, width), qmap),
        out_shape=jax.ShapeDtypeStruct((bsz, seq, dim), BF16),
        scratch_shapes=[pltpu.VMEM((2 * pairs, tq, LANES), F32), pltpu.VMEM((2 * pairs, tq, LANES), F32)],
        compiler_params=_cparams(("parallel", "parallel", "arbitrary")),
        name="sb_attention",
    )(q3, k3, v3)


RW_HALO = 8


def _head_sum_matrix():
    r = lax.broadcasted_iota(jnp.int32, (RW_DIM, RW_DIM), 0) // HEAD_DIM
    c = lax.broadcasted_iota(jnp.int32, (RW_DIM, RW_DIM), 1) // HEAD_DIM
    return jnp.where(r == c, 1.0, 0.0).astype(BF16)


def _rwkv_pre_kernel(p_ref, halo_ref, mu_ref, w0_ref, w2_ref, a0_ref, a2_ref, g2_ref, kk_ref, ka_ref,
                     r_ref, lw_ref, k_ref, v_ref, a_ref, b_ref, gate_ref):
    ts = p_ref.shape[0]
    i = pl.program_id(1)
    p = p_ref[...]
    prev_row = jnp.where(i > 0, halo_ref[RW_HALO - 1:RW_HALO, :], 0.0)
    rowid = lax.broadcasted_iota(jnp.int32, p.shape, 0)
    p_prev = jnp.where(rowid == 0, prev_row, pltpu.roll(p, 1, 0))
    ps = p + (p_prev - p) * mu_ref[...]
    r = ps[:, 0:RW_DIM]
    k = ps[:, RW_DIM:2 * RW_DIM]
    v = ps[:, 2 * RW_DIM:3 * RW_DIM]
    lora = ps[:, 3 * RW_DIM:3 * RW_DIM + LANES]
    g_lr = ps[:, 3 * RW_DIM + LANES:]
    log_w = -_softplus(-(w0_ref[...] + _dot3(jnp.tanh(lora), w2_ref[...]))) - 0.5
    lw_ref[...] = -jnp.exp(log_w)
    iclr = jax.nn.sigmoid(a0_ref[...] + _dot3(lora, a2_ref[...]))
    gate_ref[...] = _dot3(jax.nn.sigmoid(g_lr), g2_ref[...])
    kk = k * kk_ref[...]
    sumsq = _dot_exact_rhs(kk * kk, _head_sum_matrix())
    kk = kk * lax.rsqrt(jnp.maximum(sumsq, 1e-24))
    r_ref[...] = r
    k_ref[...] = k * (1.0 + (iclr - 1.0) * ka_ref[...])
    v_ref[...] = v
    a_ref[...] = -kk
    b_ref[...] = kk * iclr


def _rwkv_pre(p3, mu, w0, w2, a0, a2, g2, k_k, k_a, ts=256):
    bsz, seq, width = p3.shape
    assert seq % ts == 0 and ts % RW_HALO == 0 and width == RW_IN_DIM
    r = ts // RW_HALO
    zeros = jnp.zeros((DECAY_LORA, RW_DIM), F32)
    w2p = jnp.concatenate([w2, zeros], axis=0)
    a2p = jnp.concatenate([zeros, a2], axis=0)
    const = lambda bi, i: (0, 0)
    vec = pl.BlockSpec((1, RW_DIM), const)
    mat = pl.BlockSpec((LANES, RW_DIM), const)
    tile = pl.BlockSpec((None, ts, RW_DIM), lambda bi, i: (bi, i, 0))
    return pl.pallas_call(
        _rwkv_pre_kernel,
        grid=(bsz, seq // ts),
        in_specs=[
            pl.BlockSpec((None, ts, width), lambda bi, i: (bi, i, 0)),
            pl.BlockSpec((None, RW_HALO, width), lambda bi, i: (bi, jnp.maximum(i * r - 1, 0), 0)),
            pl.BlockSpec((1, width), const),
            vec, mat, vec, mat, mat, vec, vec,
        ],
        out_specs=[tile] * 7,
        out_shape=[jax.ShapeDtypeStruct((bsz, seq, RW_DIM), F32)] * 7,
        compiler_params=_cparams(("parallel", "parallel")),
        name="rwkv_pre",
    )(p3, p3, mu.reshape(1, -1), w0.reshape(1, -1), w2p, a0.reshape(1, -1), a2p, g2,
      k_k.reshape(1, -1), k_a.reshape(1, -1))


def _stack_heads(x, head0):
    zero = jnp.zeros_like(x)
    return jnp.concatenate([jnp.where(head0, x, zero), jnp.where(head0, zero, x)], axis=0)


def _fold_heads(x):
    n = x.shape[0] // 2
    return x[:n] + x[n:]


def _unit_lower_inverses(n_mats, same_block, eye, parts):
    nd = [jnp.where(same_block, n, 0.0) for n in n_mats]
    rest_p = [parts(n - d) for n, d in zip(n_mats, nd)]
    dinv = [eye + d for d in nd]
    p_p = [parts(d) for d in nd]
    for _ in range(3):
        p_p = [parts(_dot_parts(pp, pp)) for pp in p_p]
        dinv = [d + _dot_parts(parts(d), pp) for d, pp in zip(dinv, p_p)]
    dinv_p = [parts(d) for d in dinv]
    f = [_dot_parts(dp, rp) for dp, rp in zip(dinv_p, rest_p)]
    f_p = [parts(x) for x in f]
    f2_p = [parts(_dot_parts(fp, fp)) for fp in f_p]
    g = [eye + x for x in f]
    g = [gi + _dot_parts(parts(gi), f2) for gi, f2 in zip(g, f2_p)]
    return [_dot_parts(parts(gi), dp) for gi, dp in zip(g, dinv_p)]


def _rwkv_rec_kernel(r_ref, lw_ref, k_ref, v_ref, a_ref, b_ref, y_ref, s_ref):
    c = RW_CHUNK
    n_chunks = lw_ref.shape[0] // c
    n_pairs = RW_DIM // LANES
    parts = functools.partial(_parts, passes=RW_PASSES)
    cat = lambda *xs: jnp.concatenate(xs, axis=0)

    @pl.when(pl.program_id(1) == 0)
    def _():
        s_ref[...] = jnp.zeros_like(s_ref)

    tr = lax.broadcasted_iota(jnp.int32, (c, c), 0)
    tc = lax.broadcasted_iota(jnp.int32, (c, c), 1)
    lower = jnp.where(tr >= tc, 1.0, 0.0).astype(BF16)
    head0 = lax.broadcasted_iota(jnp.int32, (c, LANES), 1) < HEAD_DIM
    stack = functools.partial(_stack_heads, head0=head0)
    qr = lax.broadcasted_iota(jnp.int32, (4 * c, 4 * c), 0)
    qc = lax.broadcasted_iota(jnp.int32, (4 * c, 4 * c), 1)
    same_head = ((qr // c) % 2) == ((qc // c) % 2)
    t_idx, s_idx = qr % c, qc % c
    keep = same_head & ((s_idx < t_idx) | ((qr >= 2 * c) & (s_idx == t_idx)))
    pr = lax.broadcasted_iota(jnp.int32, (2 * c, 2 * c), 0)
    pc = lax.broadcasted_iota(jnp.int32, (2 * c, 2 * c), 1)
    same_block = (pr // INV_BLOCK) == (pc // INV_BLOCK)
    pair_diag = (pr // c) == (pc // c)
    eye = jnp.where(pr == pc, 1.0, 0.0)

    chains = []
    for ci in range(n_chunks):
        rows = slice(ci * c, (ci + 1) * c)
        lw = lw_ref[rows, :]
        cum = _dot_exact_lhs(lower, lw)
        last = cum[c - 1:c, :]
        a_t = parts(a_ref[rows, :] * jnp.exp(cum - lw))
        r_t = parts(r_ref[rows, :] * jnp.exp(cum))
        g_inv = jnp.exp(-cum)
        g_rem = jnp.exp(last - cum)
        b_t = parts(b_ref[rows, :] * g_inv)
        k_t = parts(k_ref[rows, :] * g_inv)
        b_h = parts(b_ref[rows, :] * g_rem)
        k_h = parts(k_ref[rows, :] * g_rem)
        v_s = parts(v_ref[rows, :])
        g_last = jnp.exp(last)
        for m in range(n_pairs):
            sl = slice(m * LANES, (m + 1) * LANES)
            pick = lambda xs: tuple(x[:, sl] for x in xs)
            chains.append(dict(rows=rows, sl=sl, a=pick(a_t), r=pick(r_t), b=pick(b_t), k=pick(k_t),
                               bh=pick(b_h), kh=pick(k_h), v=pick(v_s), g_last=g_last[:, sl]))
    for ch in chains:
        lhs = _map_parts(lambda a, r: cat(stack(a), stack(r)), ch["a"], ch["r"])
        rhs = _map_parts(lambda b, k: cat(b, b, k, k), ch["b"], ch["k"])
        ch["q"] = jnp.where(keep, _dot_parts(lhs, rhs, "nt"), 0.0)
    t_invs = _unit_lower_inverses([ch["q"][:2 * c, :2 * c] for ch in chains], same_block, eye, parts)
    for ch, t_inv in zip(chains, t_invs):
        q = ch["q"]
        ch["t_inv"] = parts(_fold_heads(t_inv))
        ch["a_ak"] = parts(_fold_heads(q[:2 * c, 2 * c:]))
        ch["a_r"] = parts(jnp.concatenate([_fold_heads(q[2 * c:, :2 * c]), _fold_heads(q[2 * c:, 2 * c:])], axis=1))
        ch["ar"] = _map_parts(cat, ch["a"], ch["r"])
        ch["v_st"] = _map_parts(stack, ch["v"])
        ch["bk"] = _map_parts(cat, ch["bh"], ch["kh"])

    state = [s_ref[m] for m in range(n_pairs)]
    for ci in range(n_chunks):
        group = chains[ci * n_pairs:(ci + 1) * n_pairs]
        s_p = [parts(s) for s in state]
        ar = [_dot_parts(ch["ar"], sp, "nt") for ch, sp in zip(group, s_p)]
        rhs_u = [x[:c] + _dot_parts(ch["a_ak"], ch["v_st"]) for ch, x in zip(group, ar)]
        u = [_dot_parts(ch["t_inv"], parts(stack(x))) for ch, x in zip(group, rhs_u)]
        u_p = [parts(x) for x in u]
        y = [x[c:] + _dot_parts(ch["a_r"], _map_parts(lambda u_, v_: cat(stack(u_), v_), up, ch["v_st"]))
             for ch, x, up in zip(group, ar, u_p)]
        outer = [_dot_parts(_map_parts(cat, up, ch["v"]), ch["bk"], "tn") for ch, up in zip(group, u_p)]
        for m, ch in enumerate(group):
            y_ref[ch["rows"], ch["sl"]] = y[m]
            state[m] = state[m] * ch["g_last"] + jnp.where(pair_diag, outer[m], 0.0)
    for m in range(n_pairs):
        s_ref[m] = state[m]


def _rwkv_recurrence(r, lw, k, v, a, b, chunks=4):
    bsz, seq, dim = r.shape
    c = RW_CHUNK * chunks
    assert seq % c == 0 and dim == RW_DIM
    tile = pl.BlockSpec((None, c, dim), lambda bi, i: (bi, i, 0))
    return pl.pallas_call(
        _rwkv_rec_kernel,
        grid=(bsz, seq // c),
        in_specs=[tile] * 6,
        out_specs=tile,
        out_shape=jax.ShapeDtypeStruct((bsz, seq, dim), F32),
        scratch_shapes=[pltpu.VMEM((dim // LANES, LANES, LANES), F32)],
        compiler_params=_cparams(("parallel", "arbitrary")),
        name="rwkv_recurrence",
    )(r, lw, k, v, a, b)


def _rwkv_post_kernel(y_ref, r_ref, k_ref, v_ref, gate_ref, rk_ref, g_ref, b_ref, o_ref):
    hsum = _head_sum_matrix()
    inv_n = 1.0 / HEAD_DIM
    y = y_ref[...]
    mu = _dot_exact_rhs(y, hsum) * inv_n
    yc = y - mu
    var = _dot_exact_rhs(yc * yc, hsum) * inv_n
    yn = yc * lax.rsqrt(var + GN_EPS) * g_ref[...] + b_ref[...]
    bonus = _dot_exact_rhs(r_ref[...] * k_ref[...] * rk_ref[...], hsum) * v_ref[...]
    o_ref[...] = ((yn + bonus) * gate_ref[...]).astype(o_ref.dtype)


def _rwkv_post(y, r, k, v, gate, r_k, lnx_g, lnx_b, tm=512):
    t, dim = y.shape
    assert t % tm == 0
    row = pl.BlockSpec((tm, dim), lambda i: (i, 0))
    vec = pl.BlockSpec((1, dim), lambda i: (0, 0))
    return pl.pallas_call(
        _rwkv_post_kernel,
        grid=(t // tm,),
        in_specs=[row] * 5 + [vec] * 3,
        out_specs=row,
        out_shape=jax.ShapeDtypeStruct((t, dim), BF16),
        compiler_params=_cparams(("parallel",)),
        name="rwkv_post",
    )(y, r, k, v, gate, r_k.reshape(1, -1), lnx_g.reshape(1, -1), lnx_b.reshape(1, -1))


POOL_HALO = 16


def _pool_kernel(u_ref, halo_ref, w_ref, b_ref, scale_ref, o_ref):
    ts = u_ref.shape[0]
    i = pl.program_id(1)
    pos = i * ts + lax.broadcasted_iota(jnp.int32, (ts, 1), 0)
    for gi, win in enumerate(POOL_WINDOWS):
        sl = slice(gi * POOL_GROUP_DIM, (gi + 1) * POOL_GROUP_DIM)
        u = u_ref[:, sl]
        halo = jnp.where(i > 0, halo_ref[:, sl], 0.0)
        ext = jnp.concatenate([halo, u], axis=0)
        shift = 1
        while shift < win:
            ext = ext + pltpu.roll(ext, shift, 0)
            shift *= 2
        count = jnp.minimum(pos + 1, win).astype(F32)
        pooled = ext[POOL_HALO:, :] / count - u
        y = _dot(pooled.astype(BF16), w_ref[gi]) + b_ref[:, sl]
        o_ref[:, sl] = (y * scale_ref[:, sl]).astype(o_ref.dtype)


def _multiscale_pool(u3, w_pool, b_pool, scale, ts=256):
    bsz, seq, dim = u3.shape
    assert seq % ts == 0 and ts % POOL_HALO == 0 and dim == POOL_DIM
    r = ts // POOL_HALO
    const = lambda bi, i: (0, 0)
    return pl.pallas_call(
        _pool_kernel,
        grid=(bsz, seq // ts),
        in_specs=[
            pl.BlockSpec((None, ts, dim), lambda bi, i: (bi, i, 0)),
            pl.BlockSpec((None, POOL_HALO, dim), lambda bi, i: (bi, jnp.maximum(i * r - 1, 0), 0)),
            pl.BlockSpec(w_pool.shape, lambda bi, i: (0, 0, 0)),
            pl.BlockSpec((1, dim), const),
            pl.BlockSpec((1, dim), const),
        ],
        out_specs=pl.BlockSpec((None, ts, dim), lambda bi, i: (bi, i, 0)),
        out_shape=jax.ShapeDtypeStruct((bsz, seq, dim), BF16),
        compiler_params=_cparams(("parallel", "parallel")),
        name="multiscale_pool",
    )(u3, u3, w_pool.astype(BF16), b_pool.reshape(1, -1), scale.reshape(1, -1))


def _even_mixer(x2, bsz, seq, w_in, w_dw, b_dw, conv_g, conv_b):
    u, q, k, v = _proj_in(x2, w_in, (2 * CONV_DIM, SB_DIM, SB_DIM, SB_DIM), (F32, BF16, BF16, BF16))
    y_conv = _conformer_conv(u.reshape(bsz, seq, -1), w_dw, b_dw, conv_g, conv_b)
    y_att = _sb_attention(q.reshape(bsz, seq, -1), k.reshape(bsz, seq, -1), v.reshape(bsz, seq, -1))
    return y_conv.reshape(bsz * seq, -1), y_att.reshape(bsz * seq, -1)


def _odd_mixer(x2, bsz, seq, w_in, mu, w0, w2, a0, a2, g2, k_k, k_a, r_k, lnx_g, lnx_b,
               w_pool, b_pool, pool_scale):
    p, u_pool = _proj_in(x2, w_in, (RW_IN_DIM, POOL_DIM), (F32, F32))
    r, lw, k, v, a, b, gate = _rwkv_pre(p.reshape(bsz, seq, -1), mu, w0, w2, a0, a2, g2, k_k, k_a)
    y = _rwkv_recurrence(r, lw, k, v, a, b)
    flat = lambda z: z.reshape(bsz * seq, -1)
    y_rw = _rwkv_post(flat(y), flat(r), flat(k), flat(v), flat(gate), r_k, lnx_g, lnx_b)
    y_pool = _multiscale_pool(u_pool.reshape(bsz, seq, -1), w_pool, b_pool, pool_scale)
    return y_rw, flat(y_pool)


def kernel(x, ffn_in, ffn_out, ln_g, ln_b, e_w_in, e_w_dw, e_b_dw, e_conv_g, e_conv_b, e_w_out, o_w_in, o_mu, o_w0, o_w2, o_a0, o_a2, o_g2, o_k_k, o_k_a, o_r_k, o_lnx_g, o_lnx_b, o_w_pool, o_b_pool, o_pool_scale, o_w_out):
    bsz, seq, d = x.shape
    x2 = x.reshape(bsz * seq, d)
    depth = ffn_in.shape[0]
    for layer in range(depth):
        x2 = _ffn_ln(x2, ffn_in[layer, 0], ffn_out[layer, 0], ln_g[layer, 0], ln_b[layer, 0])
        i = layer // 2
        if layer % 2 == 0:
            ya, yb = _even_mixer(x2, bsz, seq, e_w_in[i], e_w_dw[i], e_b_dw[i], e_conv_g[i], e_conv_b[i])
            w_out = e_w_out[i]
        else:
            ya, yb = _odd_mixer(x2, bsz, seq, o_w_in[i], o_mu[i], o_w0[i], o_w2[i], o_a0[i], o_a2[i],
                                o_g2[i], o_k_k[i], o_k_a[i], o_r_k[i], o_lnx_g[i], o_lnx_b[i],
                                o_w_pool[i], o_b_pool[i], o_pool_scale[i])
            w_out = o_w_out[i]
        x2 = _proj_out_ln(ya, yb, w_out, x2, ln_g[layer, 1], ln_b[layer, 1])
        x2 = _ffn_ln(x2, ffn_in[layer, 1], ffn_out[layer, 1], ln_g[layer, 2], ln_b[layer, 2])
    return x2.reshape(bsz, seq, d)
```

```python
import functools

import jax
import jax.numpy as jnp
from jax import lax
from jax.experimental import pallas as pl
from jax.experimental.pallas import tpu as pltpu

F32 = jnp.float32
BF16 = jnp.bfloat16

D_MODEL = 1024
DEPTH = 2
DN_ALPHA = (2.0 * DEPTH) ** 0.25
LN_EPS = 1e-5
D_FF = 2816
FFN_RES = 0.5
CONV_DIM = 512
CONV_WIDTH = 31
HEAD_DIM = 64
SB_DIM = 512
RW_DIM = 512
DECAY_LORA = 64
ICLR_LORA = 64
GATE_LORA = 128
RW_IN_DIM = 3 * RW_DIM + DECAY_LORA + ICLR_LORA + GATE_LORA
GN_EPS = HEAD_DIM * 1e-5
POOL_WINDOWS = (2, 4, 8, 16)
POOL_GROUP_DIM = 128
POOL_DIM = 512

LANES = 128
SUBLANES = 8
VMEM_LIMIT = 56 * 1024 * 1024

FFN_CHUNK = 256
RW_CHUNK = 64
RW_PASSES = 1
INV_BLOCK = 16
ATT_BLOCK = 128
ATT_DEAD = -110.0


def _cparams(sem):
    return pltpu.CompilerParams(dimension_semantics=sem, vmem_limit_bytes=VMEM_LIMIT)


def _dot(a, b, dims="nn"):
    contract = {"nn": ((1,), (0,)), "nt": ((1,), (1,)), "tn": ((0,), (0,))}[dims]
    return lax.dot_general(a, b, (contract, ((), ())), preferred_element_type=F32)


def _split2(x):
    hi = x.astype(BF16)
    lo = (x - hi.astype(F32)).astype(BF16)
    return hi, lo


def _split3(x):
    hi = x.astype(BF16)
    r = x - hi.astype(F32)
    mid = r.astype(BF16)
    lo = (r - mid.astype(F32)).astype(BF16)
    return hi, mid, lo


def _dot_exact_lhs(a_bf16, b):
    bh, bm, bl = _split3(b)
    return _dot(a_bf16, bh) + (_dot(a_bf16, bm) + _dot(a_bf16, bl))


def _head_sums(x, hsum):
    hi, lo = _split2(x)
    return _dot(hi, hsum) + _dot(lo, hsum)


def _parts(x, passes):
    return (x.astype(BF16),) if passes == 1 else _split2(x)


def _dot_parts(a, b, dims="nn"):
    out = _dot(a[0], b[0], dims)
    if len(a) > 1 and len(b) > 1:
        return out + (_dot(a[0], b[1], dims) + _dot(a[1], b[0], dims))
    if len(b) > 1:
        return out + _dot(a[0], b[1], dims)
    if len(a) > 1:
        return out + _dot(a[1], b[0], dims)
    return out


def _map_parts(fn, *parts):
    return tuple(fn(*p) for p in zip(*parts))


def _softplus(x):
    return jnp.maximum(x, 0.0) + jnp.log(1.0 + jnp.exp(-jnp.abs(x)))


def _layer_norm_rows(y, g, b, eps):
    mu = jnp.mean(y, axis=-1, keepdims=True)
    yc = y - mu
    var = jnp.mean(yc * yc, axis=-1, keepdims=True)
    return yc * lax.rsqrt(var + eps) * g + b


def _ffn_ln_math(x, win_ref, wout_ref, g, b, acc_ref):
    xb = x.astype(BF16)
    for c in range(D_FF // FFN_CHUNK):
        lo = c * FFN_CHUNK
        gate = _dot(xb, win_ref[:, lo:lo + FFN_CHUNK])
        up = _dot(xb, win_ref[:, D_FF + lo:D_FF + lo + FFN_CHUNK])
        h = (gate * jax.nn.sigmoid(gate) * up).astype(BF16)
        part = _dot(h, wout_ref[lo:lo + FFN_CHUNK, :])
        if c == 0:
            acc_ref[...] = part
        else:
            acc_ref[...] += part
    y = DN_ALPHA * x + FFN_RES * acc_ref[...]
    return _layer_norm_rows(y, g, b, LN_EPS)


def _ffn_ln_kernel(x_ref, win_ref, wout_ref, g_ref, b_ref, o_ref, acc_ref):
    o_ref[...] = _ffn_ln_math(x_ref[...], win_ref, wout_ref, g_ref[...], b_ref[...], acc_ref)


def _mix_ffn_ln_kernel(ya_ref, yb_ref, wa_ref, wb_ref, x_ref, g1_ref, b1_ref,
                       win_ref, wout_ref, g2_ref, b2_ref, o_ref, acc_ref):
    m = _dot(ya_ref[...], wa_ref[...]) + _dot(yb_ref[...], wb_ref[...])
    x = _layer_norm_rows(DN_ALPHA * x_ref[...] + m, g1_ref[...], b1_ref[...], LN_EPS)
    o_ref[...] = _ffn_ln_math(x, win_ref, wout_ref, g2_ref[...], b2_ref[...], acc_ref)


def _ffn_specs(d):
    const = lambda i: (0, 0)
    return [pl.BlockSpec((d, 2 * D_FF), const, pipeline_mode=pl.Buffered(1)),
            pl.BlockSpec((D_FF, d), const, pipeline_mode=pl.Buffered(1)),
            pl.BlockSpec((1, d), const),
            pl.BlockSpec((1, d), const)]


def _ffn_ln(x2, w_in, w_out, g, b, tm=512):
    t, d = x2.shape
    assert t % tm == 0 and D_FF % FFN_CHUNK == 0
    return pl.pallas_call(
        _ffn_ln_kernel,
        grid=(t // tm,),
        in_specs=[pl.BlockSpec((tm, d), lambda i: (i, 0))] + _ffn_specs(d),
        out_specs=pl.BlockSpec((tm, d), lambda i: (i, 0)),
        out_shape=jax.ShapeDtypeStruct((t, d), F32),
        scratch_shapes=[pltpu.VMEM((tm, d), F32)],
        compiler_params=_cparams(("parallel",)),
        name="ffn_ln",
    )(x2, w_in.astype(BF16), w_out.astype(BF16), g.reshape(1, d), b.reshape(1, d))


def _mix_ffn_ln(ya, yb, w_mix, x2, g1, b1, w_in, w_out, g2, b2, tm=512):
    t, d = x2.shape
    ka, kb = ya.shape[1], yb.shape[1]
    assert t % tm == 0 and w_mix.shape == (ka + kb, d)
    wm = w_mix.astype(BF16)
    const = lambda i: (0, 0)
    row = lambda i: (i, 0)
    vec = lambda z: z.reshape(1, d)
    return pl.pallas_call(
        _mix_ffn_ln_kernel,
        grid=(t // tm,),
        in_specs=[
            pl.BlockSpec((tm, ka), row),
            pl.BlockSpec((tm, kb), row),
            pl.BlockSpec((ka, d), const),
            pl.BlockSpec((kb, d), const),
            pl.BlockSpec((tm, d), row),
            pl.BlockSpec((1, d), const),
            pl.BlockSpec((1, d), const),
        ] + _ffn_specs(d),
        out_specs=pl.BlockSpec((tm, d), row),
        out_shape=jax.ShapeDtypeStruct((t, d), F32),
        scratch_shapes=[pltpu.VMEM((tm, d), F32)],
        compiler_params=_cparams(("parallel",)),
        name="mix_ffn_ln",
    )(ya, yb, wm[:ka], wm[ka:], x2, vec(g1), vec(b1), w_in.astype(BF16), w_out.astype(BF16), vec(g2), vec(b2))


def _proj_in_kernel(splits, x_ref, w_ref, *o_refs):
    xb = x_ref[...].astype(BF16)
    for (lo, hi), o_ref in zip(splits, o_refs):
        o_ref[...] = _dot(xb, w_ref[:, lo:hi]).astype(o_ref.dtype)


def _proj_in(x2, w, widths, dtypes, tm=512):
    t, d = x2.shape
    n = w.shape[1]
    assert t % tm == 0 and sum(widths) == n
    splits, lo = [], 0
    for wd in widths:
        splits.append((lo, lo + wd))
        lo += wd
    return pl.pallas_call(
        functools.partial(_proj_in_kernel, tuple(splits)),
        grid=(t // tm,),
        in_specs=[
            pl.BlockSpec((tm, d), lambda i: (i, 0)),
            pl.BlockSpec((d, n), lambda i: (0, 0), pipeline_mode=pl.Buffered(1)),
        ],
        out_specs=[pl.BlockSpec((tm, wd), lambda i: (i, 0)) for wd in widths],
        out_shape=[jax.ShapeDtypeStruct((t, wd), dt) for wd, dt in zip(widths, dtypes)],
        compiler_params=_cparams(("parallel",)),
        name="proj_in",
    )(x2, w.astype(BF16))


CONV_HALO = 32
CONV_ROWS = 32


def _glu(u):
    return u[:, :CONV_DIM] * jax.nn.sigmoid(u[:, CONV_DIM:])


def _conv_kernel(u_ref, halo_ref, w_ref, bdw_ref, g_ref, b_ref, o_ref, hs_ref, acc_ref):
    ts = u_ref.shape[0]
    i = pl.program_id(1)
    halo = _glu(halo_ref[...])
    hs_ref[0, 0:CONV_HALO, :] = jnp.where(i > 0, halo, 0.0)
    hs_ref[0, CONV_HALO:, :] = _glu(u_ref[...])
    span = ts + CONV_HALO - SUBLANES
    for r in range(1, SUBLANES):
        hs_ref[r, 0:span, :] = hs_ref[0, r:r + span, :]
    base = CONV_HALO - (CONV_WIDTH - 1)
    bias = jnp.broadcast_to(bdw_ref[...], (CONV_ROWS, CONV_DIM))

    def row_block(rb, carry):
        row0 = pl.multiple_of(rb * CONV_ROWS, CONV_ROWS)
        acc = bias
        for k in range(CONV_WIDTH):
            off = base + k
            window = hs_ref[off % SUBLANES, pl.ds(row0 + (off // SUBLANES) * SUBLANES, CONV_ROWS), :]
            acc = acc + w_ref[k:k + 1, :] * window
        acc_ref[pl.ds(row0, CONV_ROWS), :] = acc
        return carry

    lax.fori_loop(0, ts // CONV_ROWS, row_block, 0)
    y = _layer_norm_rows(acc_ref[...], g_ref[...], b_ref[...], LN_EPS)
    o_ref[...] = (y * jax.nn.sigmoid(y)).astype(o_ref.dtype)


def _conformer_conv(u3, w_dw, b_dw, g, b, ts=256):
    bsz, seq, two_c = u3.shape
    assert seq % ts == 0 and ts % CONV_HALO == 0 and two_c == 2 * CONV_DIM
    r = ts // CONV_HALO
    const = lambda bi, i: (0, 0)
    return pl.pallas_call(
        _conv_kernel,
        grid=(bsz, seq // ts),
        in_specs=[
            pl.BlockSpec((None, ts, two_c), lambda bi, i: (bi, i, 0)),
            pl.BlockSpec((None, CONV_HALO, two_c), lambda bi, i: (bi, jnp.maximum(i * r - 1, 0), 0)),
            pl.BlockSpec((CONV_WIDTH, CONV_DIM), const),
            pl.BlockSpec((1, CONV_DIM), const),
            pl.BlockSpec((1, CONV_DIM), const),
            pl.BlockSpec((1, CONV_DIM), const),
        ],
        out_specs=pl.BlockSpec((None, ts, CONV_DIM), lambda bi, i: (bi, i, 0)),
        out_shape=jax.ShapeDtypeStruct((bsz, seq, CONV_DIM), BF16),
        scratch_shapes=[pltpu.VMEM((SUBLANES, CONV_HALO + ts, CONV_DIM), F32), pltpu.VMEM((ts, CONV_DIM), F32)],
        compiler_params=_cparams(("parallel", "parallel")),
        name="conformer_conv",
    )(u3, u3, w_dw, b_dw.reshape(1, -1), g.reshape(1, -1), b.reshape(1, -1))


def _sb_key_block(qs, kbs, vbs, carries, u_aug, mask):
    tk = kbs[0].shape[0]
    pairs = range(len(qs))
    z = [_dot(qs[p], kbs[p], "nt") for p in pairs]
    log_keep = [-_softplus(z[p]) for p in pairs]
    if mask is not None:
        log_keep = [jnp.where(mask, lk, 0.0) for lk in log_keep]
    r = [_dot_parts(_split2(lk), (u_aug,)) for lk in log_keep]
    att = [jnp.exp(z[p] + carries[p] + r[p][:, :tk]) for p in pairs]
    if mask is not None:
        att = [jnp.where(mask, a, 0.0) for a in att]
    pv = [_dot(att[p].astype(BF16), vbs[p]) for p in pairs]
    return pv, [carries[p] + r[p][:, tk:] for p in pairs]


def _sb_attn_kernel(q_ref, k_ref, v_ref, o_ref, acc_ref, carry_ref):
    tq = q_ref.shape[0]
    tk = tq
    n_pairs = q_ref.shape[1] // LANES
    i = pl.program_id(2)
    head0 = lax.broadcasted_iota(jnp.int32, (tq, LANES), 1) < HEAD_DIM
    qs = []
    for p in range(n_pairs):
        q = (q_ref[:, p * LANES:(p + 1) * LANES].astype(F32) * (HEAD_DIM ** -0.5)).astype(BF16)
        qs.append(_stack_heads(q, head0))
    row = lax.broadcasted_iota(jnp.int32, (tk, 2 * tk), 0)
    col = lax.broadcasted_iota(jnp.int32, (tk, 2 * tk), 1)
    u_aug = jnp.where((row >= col) | (col >= tk), 1.0, 0.0).astype(BF16)
    qrow = lax.broadcasted_iota(jnp.int32, (2 * tq, tk), 0) % tq
    kcol = lax.broadcasted_iota(jnp.int32, (2 * tq, tk), 1)

    def key_block(j, first):
        start = pl.multiple_of(j * tk, tk)
        kbs = [k_ref[pl.ds(start, tk), p * LANES:(p + 1) * LANES] for p in range(n_pairs)]
        vbs = [v_ref[pl.ds(start, tk), p * LANES:(p + 1) * LANES] for p in range(n_pairs)]
        if first:
            carries = [jnp.zeros((2 * tq, LANES), F32)] * n_pairs
            pv, carries = _sb_key_block(qs, kbs, vbs, carries, u_aug, kcol < qrow)
        else:
            carries = [carry_ref[p] for p in range(n_pairs)]
            pv, carries = _sb_key_block(qs, kbs, vbs, carries, u_aug, None)
        live = None
        for p in range(n_pairs):
            if first:
                acc_ref[p] = pv[p]
            else:
                acc_ref[p] += pv[p]
            carry_ref[p] = carries[p]
            live = carries[p] if live is None else jnp.maximum(live, carries[p])
        return jnp.max(live)

    live = key_block(i, True)

    def cond(state):
        j, live = state
        return jnp.logical_and(j >= 0, live > ATT_DEAD)

    def body(state):
        j, _ = state
        return j - 1, key_block(j, False)

    lax.while_loop(cond, body, (i - 1, live))
    for p in range(n_pairs):
        o_ref[:, p * LANES:(p + 1) * LANES] = jnp.where(
            head0, acc_ref[p, :tq, :], acc_ref[p, tq:, :]).astype(o_ref.dtype)


def _sb_attention(q3, k3, v3, pairs=4):
    bsz, seq, dim = q3.shape
    tq = ATT_BLOCK
    width = pairs * LANES
    assert seq % tq == 0 and dim % width == 0
    qmap = lambda bi, hp, i: (bi, i, hp)
    kvmap = lambda bi, hp, i: (bi, 0, hp)
    return pl.pallas_call(
        _sb_attn_kernel,
        grid=(bsz, dim // width, seq // tq),
        in_specs=[
            pl.BlockSpec((None, tq, width), qmap),
            pl.BlockSpec((None, seq, width), kvmap),
            pl.BlockSpec((None, seq, width), kvmap),
        ],
        out_specs=pl.BlockSpec((None, tq, width), qmap),
        out_shape=jax.ShapeDtypeStruct((bsz, seq, dim), BF16),
        scratch_shapes=[pltpu.VMEM((pairs, 2 * tq, LANES), F32), pltpu.VMEM((pairs, 2 * tq, LANES), F32)],
        compiler_params=_cparams(("parallel", "parallel", "arbitrary")),
        name="sb_attention",
    )(q3, k3, v3)


RW_HALO = 8


def _head_sum_matrix():
    r = lax.broadcasted_iota(jnp.int32, (RW_DIM, RW_DIM), 0) // HEAD_DIM
    c = lax.broadcasted_iota(jnp.int32, (RW_DIM, RW_DIM), 1) // HEAD_DIM
    return jnp.where(r == c, 1.0, 0.0).astype(BF16)


def _rwkv_pre_math(p, prev_row, mu, w0, w2, a0, a2, g2, k_k, k_a, hsum):
    rowid = lax.broadcasted_iota(jnp.int32, p.shape, 0)
    p_prev = jnp.where(rowid == 0, prev_row, pltpu.roll(p, 1, 0))
    ps = p + (p_prev - p) * mu
    r = ps[:, 0:RW_DIM]
    k = ps[:, RW_DIM:2 * RW_DIM]
    v = ps[:, 2 * RW_DIM:3 * RW_DIM]
    lora = ps[:, 3 * RW_DIM:3 * RW_DIM + LANES]
    g_lr = ps[:, 3 * RW_DIM + LANES:]
    log_w = -_softplus(-(w0 + _dot(jnp.tanh(lora).astype(BF16), w2))) - 0.5
    lw = -jnp.exp(log_w)
    iclr = jax.nn.sigmoid(a0 + _dot(lora.astype(BF16), a2))
    gate = _dot(jax.nn.sigmoid(g_lr).astype(BF16), g2)
    kk = k * k_k
    sumsq = _head_sums(kk * kk, hsum)
    kk = kk * lax.rsqrt(jnp.maximum(sumsq, 1e-24))
    return r, lw, k * (1.0 + (iclr - 1.0) * k_a), v, -kk, kk * iclr, gate


def _stack_heads(x, head0):
    zero = jnp.zeros_like(x)
    return jnp.concatenate([jnp.where(head0, x, zero), jnp.where(head0, zero, x)], axis=0)


def _fold_heads(x):
    n = x.shape[0] // 2
    return x[:n] + x[n:]


def _unit_lower_inverses(n_mats, same_block, eye, parts):
    nd = [jnp.where(same_block, n, 0.0) for n in n_mats]
    rest_p = [parts(n - d) for n, d in zip(n_mats, nd)]
    dinv = [eye + d for d in nd]
    p_p = [parts(d) for d in nd]
    for _ in range(3):
        p_p = [parts(_dot_parts(pp, pp)) for pp in p_p]
        dinv = [d + _dot_parts(parts(d), pp) for d, pp in zip(dinv, p_p)]
    dinv_p = [parts(d) for d in dinv]
    f = [_dot_parts(dp, rp) for dp, rp in zip(dinv_p, rest_p)]
    f_p = [parts(x) for x in f]
    f2_p = [parts(_dot_parts(fp, fp)) for fp in f_p]
    g = [eye + x for x in f]
    g = [gi + _dot_parts(parts(gi), f2) for gi, f2 in zip(g, f2_p)]
    return [_dot_parts(parts(gi), dp) for gi, dp in zip(g, dinv_p)]


def _rwkv_chunk_math(r_ref, lw_ref, k_ref, v_ref, a_ref, b_ref, y_ref, s_ref):
    c = RW_CHUNK
    n_chunks = lw_ref.shape[0] // c
    n_pairs = RW_DIM // LANES
    parts = functools.partial(_parts, passes=RW_PASSES)
    cat = lambda *xs: jnp.concatenate(xs, axis=0)

    @pl.when(pl.program_id(1) == 0)
    def _():
        s_ref[...] = jnp.zeros_like(s_ref)

    tr = lax.broadcasted_iota(jnp.int32, (c, c), 0)
    tc = lax.broadcasted_iota(jnp.int32, (c, c), 1)
    lower = jnp.where(tr >= tc, 1.0, 0.0).astype(BF16)
    head0 = lax.broadcasted_iota(jnp.int32, (c, LANES), 1) < HEAD_DIM
    stack = functools.partial(_stack_heads, head0=head0)
    qr = lax.broadcasted_iota(jnp.int32, (4 * c, 4 * c), 0)
    qc = lax.broadcasted_iota(jnp.int32, (4 * c, 4 * c), 1)
    same_head = ((qr // c) % 2) == ((qc // c) % 2)
    t_idx, s_idx = qr % c, qc % c
    keep = same_head & ((s_idx < t_idx) | ((qr >= 2 * c) & (s_idx == t_idx)))
    pr = lax.broadcasted_iota(jnp.int32, (2 * c, 2 * c), 0)
    pc = lax.broadcasted_iota(jnp.int32, (2 * c, 2 * c), 1)
    same_block = (pr // INV_BLOCK) == (pc // INV_BLOCK)
    pair_diag = (pr // c) == (pc // c)
    eye = jnp.where(pr == pc, 1.0, 0.0)

    chains = []
    for ci in range(n_chunks):
        rows = slice(ci * c, (ci + 1) * c)
        lw = lw_ref[rows, :]
        cum = _dot_exact_lhs(lower, lw)
        last = cum[c - 1:c, :]
        a_t = parts(a_ref[rows, :] * jnp.exp(cum - lw))
        r_t = parts(r_ref[rows, :] * jnp.exp(cum))
        g_inv = jnp.exp(-cum)
        g_rem = jnp.exp(last - cum)
        b_t = parts(b_ref[rows, :] * g_inv)
        k_t = parts(k_ref[rows, :] * g_inv)
        b_h = parts(b_ref[rows, :] * g_rem)
        k_h = parts(k_ref[rows, :] * g_rem)
        v_s = parts(v_ref[rows, :])
        g_last = jnp.exp(last)
        for m in range(n_pairs):
            sl = slice(m * LANES, (m + 1) * LANES)
            pick = lambda xs: tuple(x[:, sl] for x in xs)
            chains.append(dict(rows=rows, sl=sl, a=pick(a_t), r=pick(r_t), b=pick(b_t), k=pick(k_t),
                               bh=pick(b_h), kh=pick(k_h), v=pick(v_s), g_last=g_last[:, sl]))
    for ch in chains:
        lhs = _map_parts(lambda a, r: cat(stack(a), stack(r)), ch["a"], ch["r"])
        rhs = _map_parts(lambda b, k: cat(b, b, k, k), ch["b"], ch["k"])
        ch["q"] = jnp.where(keep, _dot_parts(lhs, rhs, "nt"), 0.0)
    t_invs = _unit_lower_inverses([ch["q"][:2 * c, :2 * c] for ch in chains], same_block, eye, parts)
    for ch, t_inv in zip(chains, t_invs):
        q = ch["q"]
        ch["t_inv"] = parts(_fold_heads(t_inv))
        ch["a_ak"] = parts(_fold_heads(q[:2 * c, 2 * c:]))
        ch["a_r"] = parts(jnp.concatenate([_fold_heads(q[2 * c:, :2 * c]), _fold_heads(q[2 * c:, 2 * c:])], axis=1))
        ch["ar"] = _map_parts(cat, ch["a"], ch["r"])
        ch["v_st"] = _map_parts(stack, ch["v"])
        ch["bk"] = _map_parts(cat, ch["bh"], ch["kh"])

    state = [s_ref[m] for m in range(n_pairs)]
    for ci in range(n_chunks):
        group = chains[ci * n_pairs:(ci + 1) * n_pairs]
        s_p = [parts(s) for s in state]
        ar = [_dot_parts(ch["ar"], sp, "nt") for ch, sp in zip(group, s_p)]
        rhs_u = [x[:c] + _dot_parts(ch["a_ak"], ch["v_st"]) for ch, x in zip(group, ar)]
        u = [_dot_parts(ch["t_inv"], parts(stack(x))) for ch, x in zip(group, rhs_u)]
        u_p = [parts(x) for x in u]
        y = [x[c:] + _dot_parts(ch["a_r"], _map_parts(lambda u_, v_: cat(stack(u_), v_), up, ch["v_st"]))
             for ch, x, up in zip(group, ar, u_p)]
        outer = [_dot_parts(_map_parts(cat, up, ch["v"]), ch["bk"], "tn") for ch, up in zip(group, u_p)]
        for m, ch in enumerate(group):
            y_ref[ch["rows"], ch["sl"]] = y[m]
            state[m] = state[m] * ch["g_last"] + jnp.where(pair_diag, outer[m], 0.0)
    for m in range(n_pairs):
        s_ref[m] = state[m]


def _rwkv_post_math(y, r, k, v, gate, r_k, lnx_g, lnx_b, hsum):
    inv_n = 1.0 / HEAD_DIM
    mu = _head_sums(y, hsum) * inv_n
    yc = y - mu
    var = _head_sums(yc * yc, hsum) * inv_n
    yn = yc * lax.rsqrt(var + GN_EPS) * lnx_g + lnx_b
    bonus = _head_sums(r * k * r_k, hsum) * v
    return (yn + bonus) * gate


def _rwkv_mix_kernel(p_ref, halo_ref, mu_ref, w0_ref, w2_ref, a0_ref, a2_ref, g2_ref, kk_ref, ka_ref,
                     rk_ref, lnxg_ref, lnxb_ref, o_ref, y_ref, s_ref):
    hsum = _head_sum_matrix()
    prev_row = jnp.where(pl.program_id(1) > 0, halo_ref[RW_HALO - 1:RW_HALO, :], 0.0)
    r, lw, k, v, a, b, gate = _rwkv_pre_math(
        p_ref[...], prev_row, mu_ref[...], w0_ref[...], w2_ref[...], a0_ref[...], a2_ref[...], g2_ref[...],
        kk_ref[...], ka_ref[...], hsum)
    _rwkv_chunk_math(r, lw, k, v, a, b, y_ref, s_ref)
    out = _rwkv_post_math(y_ref[...], r, k, v, gate, rk_ref[...], lnxg_ref[...], lnxb_ref[...], hsum)
    o_ref[...] = out.astype(o_ref.dtype)


def _rwkv_time_mix(p3, mu, w0, w2, a0, a2, g2, k_k, k_a, r_k, lnx_g, lnx_b, chunks=4):
    bsz, seq, width = p3.shape
    ts = RW_CHUNK * chunks
    assert seq % ts == 0 and ts % RW_HALO == 0 and width == RW_IN_DIM
    r = ts // RW_HALO
    zeros = jnp.zeros((DECAY_LORA, RW_DIM), F32)
    w2p = jnp.concatenate([w2, zeros], axis=0).astype(BF16)
    a2p = jnp.concatenate([zeros, a2], axis=0).astype(BF16)
    g2 = g2.astype(BF16)
    const = lambda bi, i: (0, 0)
    vec = pl.BlockSpec((1, RW_DIM), const)
    mat = pl.BlockSpec((LANES, RW_DIM), const)
    flat = lambda z: z.reshape(1, -1)
    return pl.pallas_call(
        _rwkv_mix_kernel,
        grid=(bsz, seq // ts),
        in_specs=[
            pl.BlockSpec((None, ts, width), lambda bi, i: (bi, i, 0)),
            pl.BlockSpec((None, RW_HALO, width), lambda bi, i: (bi, jnp.maximum(i * r - 1, 0), 0)),
            pl.BlockSpec((1, width), const),
            vec, mat, vec, mat, mat, vec, vec, vec, vec, vec,
        ],
        out_specs=pl.BlockSpec((None, ts, RW_DIM), lambda bi, i: (bi, i, 0)),
        out_shape=jax.ShapeDtypeStruct((bsz, seq, RW_DIM), BF16),
        scratch_shapes=[pltpu.VMEM((ts, RW_DIM), F32), pltpu.VMEM((RW_DIM // LANES, LANES, LANES), F32)],
        compiler_params=_cparams(("parallel", "arbitrary")),
        name="rwkv_time_mix",
    )(p3, p3, flat(mu), flat(w0), w2p, flat(a0), a2p, g2, flat(k_k), flat(k_a), flat(r_k), flat(lnx_g), flat(lnx_b))


POOL_HALO = 16


def _pool_kernel(u_ref, halo_ref, w_ref, b_ref, scale_ref, o_ref):
    ts = u_ref.shape[0]
    i = pl.program_id(1)
    pos = i * ts + lax.broadcasted_iota(jnp.int32, (ts, 1), 0)
    for gi, win in enumerate(POOL_WINDOWS):
        sl = slice(gi * POOL_GROUP_DIM, (gi + 1) * POOL_GROUP_DIM)
        u = u_ref[:, sl]
        halo = jnp.where(i > 0, halo_ref[:, sl], 0.0)
        ext = jnp.concatenate([halo, u], axis=0)
        shift = 1
        while shift < win:
            ext = ext + pltpu.roll(ext, shift, 0)
            shift *= 2
        count = jnp.minimum(pos + 1, win).astype(F32)
        pooled = ext[POOL_HALO:, :] / count - u
        y = _dot(pooled.astype(BF16), w_ref[gi]) + b_ref[:, sl]
        o_ref[:, sl] = (y * scale_ref[:, sl]).astype(o_ref.dtype)


def _multiscale_pool(u3, w_pool, b_pool, scale, ts=256):
    bsz, seq, dim = u3.shape
    assert seq % ts == 0 and ts % POOL_HALO == 0 and dim == POOL_DIM
    r = ts // POOL_HALO
    const = lambda bi, i: (0, 0)
    return pl.pallas_call(
        _pool_kernel,
        grid=(bsz, seq // ts),
        in_specs=[
            pl.BlockSpec((None, ts, dim), lambda bi, i: (bi, i, 0)),
            pl.BlockSpec((None, POOL_HALO, dim), lambda bi, i: (bi, jnp.maximum(i * r - 1, 0), 0)),
            pl.BlockSpec(w_pool.shape, lambda bi, i: (0, 0, 0)),
            pl.BlockSpec((1, dim), const),
            pl.BlockSpec((1, dim), const),
        ],
        out_specs=pl.BlockSpec((None, ts, dim), lambda bi, i: (bi, i, 0)),
        out_shape=jax.ShapeDtypeStruct((bsz, seq, dim), BF16),
        compiler_params=_cparams(("parallel", "parallel")),
        name="multiscale_pool",
    )(u3, u3, w_pool.astype(BF16), b_pool.reshape(1, -1), scale.reshape(1, -1))


def _even_mixer(x2, bsz, seq, w_in, w_dw, b_dw, conv_g, conv_b):
    u, q, k, v = _proj_in(x2, w_in, (2 * CONV_DIM, SB_DIM, SB_DIM, SB_DIM), (F32, BF16, BF16, BF16))
    y_conv = _conformer_conv(u.reshape(bsz, seq, -1), w_dw, b_dw, conv_g, conv_b)
    y_att = _sb_attention(q.reshape(bsz, seq, -1), k.reshape(bsz, seq, -1), v.reshape(bsz, seq, -1))
    return y_conv.reshape(bsz * seq, -1), y_att.reshape(bsz * seq, -1)


def _odd_mixer(x2, bsz, seq, w_in, mu, w0, w2, a0, a2, g2, k_k, k_a, r_k, lnx_g, lnx_b,
               w_pool, b_pool, pool_scale):
    p, u_pool = _proj_in(x2, w_in, (RW_IN_DIM, POOL_DIM), (F32, F32))
    y_rw = _rwkv_time_mix(p.reshape(bsz, seq, -1), mu, w0, w2, a0, a2, g2, k_k, k_a, r_k, lnx_g, lnx_b)
    y_pool = _multiscale_pool(u_pool.reshape(bsz, seq, -1), w_pool, b_pool, pool_scale)
    flat = lambda z: z.reshape(bsz * seq, -1)
    return flat(y_rw), flat(y_pool)


def kernel(x, ffn_in, ffn_out, ln_g, ln_b, e_w_in, e_w_dw, e_b_dw, e_conv_g, e_conv_b, e_w_out, o_w_in, o_mu, o_w0, o_w2, o_a0, o_a2, o_g2, o_k_k, o_k_a, o_r_k, o_lnx_g, o_lnx_b, o_w_pool, o_b_pool, o_pool_scale, o_w_out):
    bsz, seq, d = x.shape
    x2 = x.reshape(bsz * seq, d)
    depth = ffn_in.shape[0]
    for layer in range(depth):
        x2 = _ffn_ln(x2, ffn_in[layer, 0], ffn_out[layer, 0], ln_g[layer, 0], ln_b[layer, 0])
        i = layer // 2
        if layer % 2 == 0:
            ya, yb = _even_mixer(x2, bsz, seq, e_w_in[i], e_w_dw[i], e_b_dw[i], e_conv_g[i], e_conv_b[i])
            w_out = e_w_out[i]
        else:
            ya, yb = _odd_mixer(x2, bsz, seq, o_w_in[i], o_mu[i], o_w0[i], o_w2[i], o_a0[i], o_a2[i],
                                o_g2[i], o_k_k[i], o_k_a[i], o_r_k[i], o_lnx_g[i], o_lnx_b[i],
                                o_w_pool[i], o_b_pool[i], o_pool_scale[i])
            w_out = o_w_out[i]
        x2 = _mix_ffn_ln(ya, yb, w_out, x2, ln_g[layer, 1], ln_b[layer, 1],
                         ffn_in[layer, 1], ffn_out[layer, 1], ln_g[layer, 2], ln_b[layer, 2])
    return x2.reshape(bsz, seq, d)
```

```python
import functools

import jax
import jax.numpy as jnp
from jax import lax
from jax.experimental import pallas as pl
from jax.experimental.pallas import tpu as pltpu

F32 = jnp.float32
BF16 = jnp.bfloat16

D_MODEL = 1024
DEPTH = 2
DN_ALPHA = (2.0 * DEPTH) ** 0.25
LN_EPS = 1e-5
D_FF = 2816
FFN_RES = 0.5
CONV_DIM = 512
CONV_WIDTH = 31
HEAD_DIM = 64
SB_DIM = 512
RW_DIM = 512
DECAY_LORA = 64
ICLR_LORA = 64
GATE_LORA = 128
RW_IN_DIM = 3 * RW_DIM + DECAY_LORA + ICLR_LORA + GATE_LORA
GN_EPS = HEAD_DIM * 1e-5
POOL_WINDOWS = (2, 4, 8, 16)
POOL_GROUP_DIM = 128
POOL_DIM = 512

LANES = 128
SUBLANES = 8
VMEM_LIMIT = 56 * 1024 * 1024

FFN_CHUNK = 256
RW_CHUNK = 64
RW_PASSES = 1
INV_BLOCK = 16
ATT_BLOCK = 128
ATT_DEAD = -110.0


def _cparams(sem):
    return pltpu.CompilerParams(dimension_semantics=sem, vmem_limit_bytes=VMEM_LIMIT)


def _dot(a, b, dims="nn"):
    contract = {"nn": ((1,), (0,)), "nt": ((1,), (1,)), "tn": ((0,), (0,))}[dims]
    return lax.dot_general(a, b, (contract, ((), ())), preferred_element_type=F32)


def _split2(x):
    hi = x.astype(BF16)
    lo = (x - hi.astype(F32)).astype(BF16)
    return hi, lo


def _split3(x):
    hi = x.astype(BF16)
    r = x - hi.astype(F32)
    mid = r.astype(BF16)
    lo = (r - mid.astype(F32)).astype(BF16)
    return hi, mid, lo


def _dot_exact_lhs(a_bf16, b):
    bh, bm, bl = _split3(b)
    return _dot(a_bf16, bh) + (_dot(a_bf16, bm) + _dot(a_bf16, bl))


def _head_sums(x, hsum):
    hi, lo = _split2(x)
    return _dot(hi, hsum) + _dot(lo, hsum)


def _parts(x, passes):
    return (x.astype(BF16),) if passes == 1 else _split2(x)


def _dot_parts(a, b, dims="nn"):
    out = _dot(a[0], b[0], dims)
    if len(a) > 1 and len(b) > 1:
        return out + (_dot(a[0], b[1], dims) + _dot(a[1], b[0], dims))
    if len(b) > 1:
        return out + _dot(a[0], b[1], dims)
    if len(a) > 1:
        return out + _dot(a[1], b[0], dims)
    return out


def _map_parts(fn, *parts):
    return tuple(fn(*p) for p in zip(*parts))


def _softplus(x):
    return jnp.maximum(x, 0.0) + jnp.log(1.0 + jnp.exp(-jnp.abs(x)))


def _layer_norm_rows(y, g, b, eps):
    mu = jnp.mean(y, axis=-1, keepdims=True)
    yc = y - mu
    var = jnp.mean(yc * yc, axis=-1, keepdims=True)
    return yc * lax.rsqrt(var + eps) * g + b


def _ffn_ln_math(x, win_ref, wout_ref, g, b, acc_ref):
    xb = x.astype(BF16)
    for c in range(D_FF // FFN_CHUNK):
        lo = c * FFN_CHUNK
        gate = _dot(xb, win_ref[:, lo:lo + FFN_CHUNK])
        up = _dot(xb, win_ref[:, D_FF + lo:D_FF + lo + FFN_CHUNK])
        h = (gate * jax.nn.sigmoid(gate) * up).astype(BF16)
        part = _dot(h, wout_ref[lo:lo + FFN_CHUNK, :])
        if c == 0:
            acc_ref[...] = part
        else:
            acc_ref[...] += part
    y = DN_ALPHA * x + FFN_RES * acc_ref[...]
    return _layer_norm_rows(y, g, b, LN_EPS)


def _ffn_ln_kernel(x_ref, win_ref, wout_ref, g_ref, b_ref, o_ref, acc_ref):
    o_ref[...] = _ffn_ln_math(x_ref[...], win_ref, wout_ref, g_ref[...], b_ref[...], acc_ref)


def _mix_ffn_ln_kernel(ya_ref, yb_ref, wa_ref, wb_ref, x_ref, g1_ref, b1_ref,
                       win_ref, wout_ref, g2_ref, b2_ref, o_ref, acc_ref):
    m = _dot(ya_ref[...], wa_ref[...]) + _dot(yb_ref[...], wb_ref[...])
    x = _layer_norm_rows(DN_ALPHA * x_ref[...] + m, g1_ref[...], b1_ref[...], LN_EPS)
    o_ref[...] = _ffn_ln_math(x, win_ref, wout_ref, g2_ref[...], b2_ref[...], acc_ref)


def _ffn_specs(d, which):
    const = lambda i: (0, 0)
    pick = lambda i: (which[0], which[1], 0, 0)
    return [pl.BlockSpec((None, None, d, 2 * D_FF), pick, pipeline_mode=pl.Buffered(1)),
            pl.BlockSpec((None, None, D_FF, d), pick, pipeline_mode=pl.Buffered(1)),
            pl.BlockSpec((1, d), const),
            pl.BlockSpec((1, d), const)]


def _ffn_ln(x2, w_in_all, w_out_all, which, g, b, tm=512):
    t, d = x2.shape
    assert t % tm == 0 and D_FF % FFN_CHUNK == 0
    return pl.pallas_call(
        _ffn_ln_kernel,
        grid=(t // tm,),
        in_specs=[pl.BlockSpec((tm, d), lambda i: (i, 0))] + _ffn_specs(d, which),
        out_specs=pl.BlockSpec((tm, d), lambda i: (i, 0)),
        out_shape=jax.ShapeDtypeStruct((t, d), F32),
        scratch_shapes=[pltpu.VMEM((tm, d), F32)],
        compiler_params=_cparams(("parallel",)),
        name="ffn_ln",
    )(x2, w_in_all, w_out_all, g.reshape(1, d), b.reshape(1, d))


def _mix_ffn_ln(ya, yb, w_mix, x2, g1, b1, w_in_all, w_out_all, which, g2, b2, tm=512):
    t, d = x2.shape
    ka, kb = ya.shape[1], yb.shape[1]
    assert t % tm == 0 and ka == kb and w_mix.shape == (ka + kb, d)
    wm = w_mix.astype(BF16)
    const = lambda i: (0, 0)
    row = lambda i: (i, 0)
    vec = lambda z: z.reshape(1, d)
    return pl.pallas_call(
        _mix_ffn_ln_kernel,
        grid=(t // tm,),
        in_specs=[
            pl.BlockSpec((tm, ka), row),
            pl.BlockSpec((tm, kb), row),
            pl.BlockSpec((ka, d), const),
            pl.BlockSpec((kb, d), lambda i: (1, 0)),
            pl.BlockSpec((tm, d), row),
            pl.BlockSpec((1, d), const),
            pl.BlockSpec((1, d), const),
        ] + _ffn_specs(d, which),
        out_specs=pl.BlockSpec((tm, d), row),
        out_shape=jax.ShapeDtypeStruct((t, d), F32),
        scratch_shapes=[pltpu.VMEM((tm, d), F32)],
        compiler_params=_cparams(("parallel",)),
        name="mix_ffn_ln",
    )(ya, yb, wm, wm, x2, vec(g1), vec(b1), w_in_all, w_out_all, vec(g2), vec(b2))


def _proj_in_kernel(splits, x_ref, w_ref, *o_refs):
    xb = x_ref[...].astype(BF16)
    for (lo, hi), o_ref in zip(splits, o_refs):
        o_ref[...] = _dot(xb, w_ref[:, lo:hi]).astype(o_ref.dtype)


def _proj_in(x2, w, widths, dtypes, tm=512):
    t, d = x2.shape
    n = w.shape[1]
    assert t % tm == 0 and sum(widths) == n
    splits, lo = [], 0
    for wd in widths:
        splits.append((lo, lo + wd))
        lo += wd
    return pl.pallas_call(
        functools.partial(_proj_in_kernel, tuple(splits)),
        grid=(t // tm,),
        in_specs=[
            pl.BlockSpec((tm, d), lambda i: (i, 0)),
            pl.BlockSpec((d, n), lambda i: (0, 0), pipeline_mode=pl.Buffered(1)),
        ],
        out_specs=[pl.BlockSpec((tm, wd), lambda i: (i, 0)) for wd in widths],
        out_shape=[jax.ShapeDtypeStruct((t, wd), dt) for wd, dt in zip(widths, dtypes)],
        compiler_params=_cparams(("parallel",)),
        name="proj_in",
    )(x2, w.astype(BF16))


CONV_HALO = 32
CONV_ROWS = 128


def _glu(u):
    return u[:, :CONV_DIM] * jax.nn.sigmoid(u[:, CONV_DIM:])


def _conv_kernel(u_ref, halo_ref, w_ref, bdw_ref, g_ref, b_ref, o_ref, hs_ref, acc_ref):
    ts = u_ref.shape[0]
    i = pl.program_id(1)
    halo = _glu(halo_ref[...])
    hs_ref[0, 0:CONV_HALO, :] = jnp.where(i > 0, halo, 0.0)
    hs_ref[0, CONV_HALO:, :] = _glu(u_ref[...])
    span = ts + CONV_HALO - SUBLANES
    for r in range(1, SUBLANES):
        hs_ref[r, 0:span, :] = hs_ref[0, r:r + span, :]
    base = CONV_HALO - (CONV_WIDTH - 1)
    bias = jnp.broadcast_to(bdw_ref[...], (CONV_ROWS, CONV_DIM))

    def row_block(rb, carry):
        row0 = pl.multiple_of(rb * CONV_ROWS, CONV_ROWS)
        acc = bias
        for k in range(CONV_WIDTH):
            off = base + k
            window = hs_ref[off % SUBLANES, pl.ds(row0 + (off // SUBLANES) * SUBLANES, CONV_ROWS), :]
            acc = acc + w_ref[k:k + 1, :] * window
        acc_ref[pl.ds(row0, CONV_ROWS), :] = acc
        return carry

    lax.fori_loop(0, ts // CONV_ROWS, row_block, 0)
    y = _layer_norm_rows(acc_ref[...], g_ref[...], b_ref[...], LN_EPS)
    o_ref[...] = (y * jax.nn.sigmoid(y)).astype(o_ref.dtype)


def _conformer_conv(u3, w_dw, b_dw, g, b, ts=256):
    bsz, seq, two_c = u3.shape
    assert seq % ts == 0 and ts % CONV_HALO == 0 and two_c == 2 * CONV_DIM
    r = ts // CONV_HALO
    const = lambda bi, i: (0, 0)
    return pl.pallas_call(
        _conv_kernel,
        grid=(bsz, seq // ts),
        in_specs=[
            pl.BlockSpec((None, ts, two_c), lambda bi, i: (bi, i, 0)),
            pl.BlockSpec((None, CONV_HALO, two_c), lambda bi, i: (bi, jnp.maximum(i * r - 1, 0), 0)),
            pl.BlockSpec((CONV_WIDTH, CONV_DIM), const),
            pl.BlockSpec((1, CONV_DIM), const),
            pl.BlockSpec((1, CONV_DIM), const),
            pl.BlockSpec((1, CONV_DIM), const),
        ],
        out_specs=pl.BlockSpec((None, ts, CONV_DIM), lambda bi, i: (bi, i, 0)),
        out_shape=jax.ShapeDtypeStruct((bsz, seq, CONV_DIM), BF16),
        scratch_shapes=[pltpu.VMEM((SUBLANES, CONV_HALO + ts, CONV_DIM), F32), pltpu.VMEM((ts, CONV_DIM), F32)],
        compiler_params=_cparams(("parallel", "parallel")),
        name="conformer_conv",
    )(u3, u3, w_dw, b_dw.reshape(1, -1), g.reshape(1, -1), b.reshape(1, -1))


def _sb_key_block(qs, kbs, vbs, carries, u_aug, mask):
    tk = kbs[0].shape[0]
    pairs = range(len(qs))
    z = [_dot(qs[p], kbs[p], "nt") for p in pairs]
    log_keep = [-_softplus(z[p]) for p in pairs]
    if mask is not None:
        log_keep = [jnp.where(mask, lk, 0.0) for lk in log_keep]
    r = [_dot_parts(_split2(lk), (u_aug,)) for lk in log_keep]
    att = [jnp.exp(z[p] + carries[p] + r[p][:, :tk]) for p in pairs]
    if mask is not None:
        att = [jnp.where(mask, a, 0.0) for a in att]
    pv = [_dot(att[p].astype(BF16), vbs[p]) for p in pairs]
    return pv, [carries[p] + r[p][:, tk:] for p in pairs]


def _sb_attn_kernel(q_ref, k_ref, v_ref, o_ref, acc_ref, carry_ref):
    tq = q_ref.shape[0]
    tk = tq
    n_pairs = q_ref.shape[1] // LANES
    i = pl.program_id(2)
    head0 = lax.broadcasted_iota(jnp.int32, (tq, LANES), 1) < HEAD_DIM
    qs = []
    for p in range(n_pairs):
        q = (q_ref[:, p * LANES:(p + 1) * LANES].astype(F32) * (HEAD_DIM ** -0.5)).astype(BF16)
        qs.append(_stack_heads(q, head0))
    row = lax.broadcasted_iota(jnp.int32, (tk, 2 * tk), 0)
    col = lax.broadcasted_iota(jnp.int32, (tk, 2 * tk), 1)
    u_aug = jnp.where((row >= col) | (col >= tk), 1.0, 0.0).astype(BF16)
    qrow = lax.broadcasted_iota(jnp.int32, (2 * tq, tk), 0) % tq
    kcol = lax.broadcasted_iota(jnp.int32, (2 * tq, tk), 1)

    def key_block(j, first):
        start = pl.multiple_of(j * tk, tk)
        kbs = [k_ref[pl.ds(start, tk), p * LANES:(p + 1) * LANES] for p in range(n_pairs)]
        vbs = [v_ref[pl.ds(start, tk), p * LANES:(p + 1) * LANES] for p in range(n_pairs)]
        if first:
            carries = [jnp.zeros((2 * tq, LANES), F32)] * n_pairs
            pv, carries = _sb_key_block(qs, kbs, vbs, carries, u_aug, kcol < qrow)
        else:
            carries = [carry_ref[p] for p in range(n_pairs)]
            pv, carries = _sb_key_block(qs, kbs, vbs, carries, u_aug, None)
        live = None
        for p in range(n_pairs):
            if first:
                acc_ref[p] = pv[p]
            else:
                acc_ref[p] += pv[p]
            carry_ref[p] = carries[p]
            live = carries[p] if live is None else jnp.maximum(live, carries[p])
        return jnp.max(live)

    live = key_block(i, True)

    def cond(state):
        j, live = state
        return jnp.logical_and(j >= 0, live > ATT_DEAD)

    def body(state):
        j, _ = state
        return j - 1, key_block(j, False)

    lax.while_loop(cond, body, (i - 1, live))
    for p in range(n_pairs):
        o_ref[:, p * LANES:(p + 1) * LANES] = jnp.where(
            head0, acc_ref[p, :tq, :], acc_ref[p, tq:, :]).astype(o_ref.dtype)


def _sb_attention(q3, k3, v3, pairs=4):
    bsz, seq, dim = q3.shape
    tq = ATT_BLOCK
    width = pairs * LANES
    assert seq % tq == 0 and dim % width == 0
    qmap = lambda bi, hp, i: (bi, i, hp)
    kvmap = lambda bi, hp, i: (bi, 0, hp)
    return pl.pallas_call(
        _sb_attn_kernel,
        grid=(bsz, dim // width, seq // tq),
        in_specs=[
            pl.BlockSpec((None, tq, width), qmap),
            pl.BlockSpec((None, seq, width), kvmap),
            pl.BlockSpec((None, seq, width), kvmap),
        ],
        out_specs=pl.BlockSpec((None, tq, width), qmap),
        out_shape=jax.ShapeDtypeStruct((bsz, seq, dim), BF16),
        scratch_shapes=[pltpu.VMEM((pairs, 2 * tq, LANES), F32), pltpu.VMEM((pairs, 2 * tq, LANES), F32)],
        compiler_params=_cparams(("parallel", "parallel", "arbitrary")),
        name="sb_attention",
    )(q3, k3, v3)


RW_HALO = 8


def _rwkv_pre_math(p, prev_row, mu, w0, w2, a0, a2, g2, k_k, k_a, hsum):
    rowid = lax.broadcasted_iota(jnp.int32, p.shape, 0)
    p_prev = jnp.where(rowid == 0, prev_row, pltpu.roll(p, 1, 0))
    ps = p + (p_prev - p) * mu
    r = ps[:, 0:RW_DIM]
    k = ps[:, RW_DIM:2 * RW_DIM]
    v = ps[:, 2 * RW_DIM:3 * RW_DIM]
    lora = ps[:, 3 * RW_DIM:3 * RW_DIM + LANES]
    g_lr = ps[:, 3 * RW_DIM + LANES:]
    log_w = -_softplus(-(w0 + _dot(jnp.tanh(lora).astype(BF16), w2))) - 0.5
    lw = -jnp.exp(log_w)
    iclr = jax.nn.sigmoid(a0 + _dot(lora.astype(BF16), a2))
    gate = _dot(jax.nn.sigmoid(g_lr).astype(BF16), g2)
    kk = k * k_k
    sumsq = _head_sums(kk * kk, hsum)
    kk = kk * lax.rsqrt(jnp.maximum(sumsq, 1e-24))
    return r, lw, k * (1.0 + (iclr - 1.0) * k_a), v, -kk, kk * iclr, gate


def _stack_heads(x, head0):
    zero = jnp.zeros_like(x)
    return jnp.concatenate([jnp.where(head0, x, zero), jnp.where(head0, zero, x)], axis=0)


def _fold_heads(x):
    n = x.shape[0] // 2
    return x[:n] + x[n:]


def _unit_lower_inverses(n_mats, same_block, eye, parts):
    nd = [jnp.where(same_block, n, 0.0) for n in n_mats]
    rest_p = [parts(n - d) for n, d in zip(n_mats, nd)]
    dinv = [eye + d for d in nd]
    p_p = [parts(d) for d in nd]
    for _ in range(3):
        p_p = [parts(_dot_parts(pp, pp)) for pp in p_p]
        dinv = [d + _dot_parts(parts(d), pp) for d, pp in zip(dinv, p_p)]
    dinv_p = [parts(d) for d in dinv]
    f = [_dot_parts(dp, rp) for dp, rp in zip(dinv_p, rest_p)]
    f_p = [parts(x) for x in f]
    f2_p = [parts(_dot_parts(fp, fp)) for fp in f_p]
    g = [eye + x for x in f]
    g = [gi + _dot_parts(parts(gi), f2) for gi, f2 in zip(g, f2_p)]
    return [_dot_parts(parts(gi), dp) for gi, dp in zip(g, dinv_p)]


def _rwkv_masks():
    c = RW_CHUNK
    qr = jnp.arange(4 * c)[:, None]
    qc = jnp.arange(4 * c)[None, :]
    same_head = ((qr // c) % 2) == ((qc // c) % 2)
    t_idx, s_idx = qr % c, qc % c
    keep = same_head & ((s_idx < t_idx) | ((qr >= 2 * c) & (s_idx == t_idx)))
    pr = jnp.arange(2 * c)[:, None]
    pc = jnp.arange(2 * c)[None, :]
    same_block = (pr // INV_BLOCK) == (pc // INV_BLOCK)
    pair_diag = (pr // c) == (pc // c)
    eye = pr == pc
    right = jnp.concatenate([jnp.concatenate([same_block, pair_diag], axis=1),
                             jnp.concatenate([eye, jnp.zeros_like(eye)], axis=1)], axis=0)
    return jnp.concatenate([keep, right], axis=1).astype(F32)


def _unpack_masks(masks):
    n = 2 * RW_CHUNK
    return dict(keep=masks[:, :2 * n] > 0.5, same_block=masks[:n, 2 * n:3 * n] > 0.5,
                pair_diag=masks[:n, 3 * n:] > 0.5, eye=masks[n:, 2 * n:3 * n])


def _rwkv_chunk_prepare(r_ref, lw_ref, k_ref, v_ref, a_ref, b_ref, masks):
    c = RW_CHUNK
    n_chunks = lw_ref.shape[0] // c
    n_pairs = RW_DIM // LANES
    parts = functools.partial(_parts, passes=RW_PASSES)
    cat = lambda *xs: jnp.concatenate(xs, axis=0)
    tr = lax.broadcasted_iota(jnp.int32, (c, c), 0)
    tc = lax.broadcasted_iota(jnp.int32, (c, c), 1)
    lower = jnp.where(tr >= tc, 1.0, 0.0).astype(BF16)
    head0 = lax.broadcasted_iota(jnp.int32, (c, LANES), 1) < HEAD_DIM
    stack = functools.partial(_stack_heads, head0=head0)
    keep, same_block, eye, pair_diag = masks["keep"], masks["same_block"], masks["eye"], masks["pair_diag"]

    chains = []
    for ci in range(n_chunks):
        rows = slice(ci * c, (ci + 1) * c)
        lw = lw_ref[rows, :]
        cum = _dot_exact_lhs(lower, lw)
        last = cum[c - 1:c, :]
        a_t = parts(a_ref[rows, :] * jnp.exp(cum - lw))
        r_f32 = r_ref[rows, :] * jnp.exp(cum)
        r_t = parts(r_f32)
        g_inv = jnp.exp(-cum)
        g_rem = jnp.exp(last - cum)
        b_t = parts(b_ref[rows, :] * g_inv)
        k_t = parts(k_ref[rows, :] * g_inv)
        b_h = parts(b_ref[rows, :] * g_rem)
        k_h = parts(k_ref[rows, :] * g_rem)
        v_s = parts(v_ref[rows, :])
        g_last = jnp.exp(last)
        for m in range(n_pairs):
            sl = slice(m * LANES, (m + 1) * LANES)
            pick = lambda xs: tuple(x[:, sl] for x in xs)
            chains.append(dict(rows=rows, sl=sl, a=pick(a_t), r=pick(r_t), b=pick(b_t), k=pick(k_t),
                               bh=pick(b_h), kh=pick(k_h), v=pick(v_s), g_last=g_last[:, sl],
                               r_f32=r_f32[:, sl]))
    for ch in chains:
        lhs = _map_parts(lambda a, r: cat(stack(a), stack(r)), ch["a"], ch["r"])
        rhs = _map_parts(lambda b, k: cat(b, b, k, k), ch["b"], ch["k"])
        ch["q"] = jnp.where(keep, _dot_parts(lhs, rhs, "nt"), 0.0)
    t_invs = _unit_lower_inverses([ch["q"][:2 * c, :2 * c] for ch in chains], same_block, eye, parts)
    for ch, t_inv in zip(chains, t_invs):
        q = ch["q"]
        ch["t_inv"] = parts(_fold_heads(t_inv))
        ch["a_rb"] = parts(_fold_heads(q[2 * c:, :2 * c]))
        ch["a_r"] = parts(jnp.concatenate([_fold_heads(q[2 * c:, :2 * c]), _fold_heads(q[2 * c:, 2 * c:])], axis=1))
        ch["v_st"] = _map_parts(stack, ch["v"])
        ch["w"] = _dot_parts(ch["t_inv"], _map_parts(stack, ch["a"]))
        ch["rhs0"] = _dot_parts(parts(_fold_heads(q[:2 * c, 2 * c:])), ch["v_st"])
    for ch in chains:
        ch["x_p"] = parts(_dot_parts(ch["t_inv"], parts(stack(ch["rhs0"]))))
        ch["w_p"] = parts(ch["w"])
    for ch in chains:
        r_eff = ch["r_f32"] + _dot_parts(ch["a_rb"], _map_parts(stack, ch["w_p"]))
        ch["r_eff"] = parts(r_eff)
        ch["y0"] = _dot_parts(ch["a_r"], _map_parts(lambda x_, v_: cat(stack(x_), v_), ch["x_p"], ch["v_st"]))
        ch["m"] = parts(jnp.where(pair_diag, _dot_parts(ch["w_p"], ch["bh"], "tn"), 0.0))
        ch["c"] = jnp.where(pair_diag, _dot_parts(_map_parts(cat, ch["x_p"], ch["v"]),
                                                  _map_parts(cat, ch["bh"], ch["kh"]), "tn"), 0.0)
    return chains


def _rwkv_chunk_chain(chains, y_ref, s_ref):
    n_pairs = RW_DIM // LANES
    n_chunks = len(chains) // n_pairs
    parts = functools.partial(_parts, passes=RW_PASSES)
    state = [s_ref[m] for m in range(n_pairs)]
    for ci in range(n_chunks):
        group = chains[ci * n_pairs:(ci + 1) * n_pairs]
        s_p = [parts(s) for s in state]
        for m, ch in enumerate(group):
            y_ref[ch["rows"], ch["sl"]] = _dot_parts(ch["r_eff"], s_p[m], "nt") + ch["y0"]
            state[m] = state[m] * ch["g_last"] + (_dot_parts(s_p[m], ch["m"]) + ch["c"])
    for m in range(n_pairs):
        s_ref[m] = state[m]


def _rwkv_post_math(y, r, k, v, gate, r_k, lnx_g, lnx_b, hsum):
    inv_n = 1.0 / HEAD_DIM
    mu = _head_sums(y, hsum) * inv_n
    yc = y - mu
    var = _head_sums(yc * yc, hsum) * inv_n
    yn = yc * lax.rsqrt(var + GN_EPS) * lnx_g + lnx_b
    bonus = _head_sums(r * k * r_k, hsum) * v
    return (yn + bonus) * gate


def _rwkv_mix_kernel(p_ref, halo_ref, mu_ref, w0_ref, w2_ref, a0_ref, a2_ref, g2_ref, kk_ref, ka_ref,
                     rk_ref, lnxg_ref, lnxb_ref, hsum_ref, masks_ref, o_ref, y_ref, s_ref):
    @pl.when(pl.program_id(1) == 0)
    def _():
        s_ref[...] = jnp.zeros_like(s_ref)

    hsum = hsum_ref[...]
    masks = _unpack_masks(masks_ref[...])
    prev_row = jnp.where(pl.program_id(1) > 0, halo_ref[RW_HALO - 1:RW_HALO, :], 0.0)
    r, lw, k, v, a, b, gate = _rwkv_pre_math(
        p_ref[...], prev_row, mu_ref[...], w0_ref[...], w2_ref[...], a0_ref[...], a2_ref[...], g2_ref[...],
        kk_ref[...], ka_ref[...], hsum)
    _rwkv_chunk_chain(_rwkv_chunk_prepare(r, lw, k, v, a, b, masks), y_ref, s_ref)
    out = _rwkv_post_math(y_ref[...], r, k, v, gate, rk_ref[...], lnxg_ref[...], lnxb_ref[...], hsum)
    o_ref[...] = out.astype(o_ref.dtype)


def _rwkv_time_mix(p3, mu, w0, w2, a0, a2, g2, k_k, k_a, r_k, lnx_g, lnx_b, chunks=4):
    bsz, seq, width = p3.shape
    ts = RW_CHUNK * chunks
    assert seq % ts == 0 and ts % RW_HALO == 0 and width == RW_IN_DIM
    r = ts // RW_HALO
    head_of = jnp.arange(RW_DIM) // HEAD_DIM
    hsum = (head_of[:, None] == head_of[None, :]).astype(BF16)
    zeros = jnp.zeros((DECAY_LORA, RW_DIM), F32)
    w2p = jnp.concatenate([w2, zeros], axis=0).astype(BF16)
    a2p = jnp.concatenate([zeros, a2], axis=0).astype(BF16)
    g2 = g2.astype(BF16)
    const = lambda bi, i: (0, 0)
    vec = pl.BlockSpec((1, RW_DIM), const)
    mat = pl.BlockSpec((LANES, RW_DIM), const)
    flat = lambda z: z.reshape(1, -1)
    return pl.pallas_call(
        _rwkv_mix_kernel,
        grid=(bsz, seq // ts),
        in_specs=[
            pl.BlockSpec((None, ts, width), lambda bi, i: (bi, i, 0)),
            pl.BlockSpec((None, RW_HALO, width), lambda bi, i: (bi, jnp.maximum(i * r - 1, 0), 0)),
            pl.BlockSpec((1, width), const),
            vec, mat, vec, mat, mat, vec, vec, vec, vec, vec,
            pl.BlockSpec((RW_DIM, RW_DIM), const),
            pl.BlockSpec((4 * RW_CHUNK, 8 * RW_CHUNK), const),
        ],
        out_specs=pl.BlockSpec((None, ts, RW_DIM), lambda bi, i: (bi, i, 0)),
        out_shape=jax.ShapeDtypeStruct((bsz, seq, RW_DIM), BF16),
        scratch_shapes=[pltpu.VMEM((ts, RW_DIM), F32), pltpu.VMEM((RW_DIM // LANES, LANES, LANES), F32)],
        compiler_params=_cparams(("parallel", "arbitrary")),
        name="rwkv_time_mix",
    )(p3, p3, flat(mu), flat(w0), w2p, flat(a0), a2p, g2, flat(k_k), flat(k_a), flat(r_k), flat(lnx_g), flat(lnx_b),
      hsum, _rwkv_masks())


POOL_HALO = 16


def _pool_kernel(u_ref, halo_ref, w_ref, b_ref, scale_ref, o_ref):
    ts = u_ref.shape[0]
    i = pl.program_id(1)
    pos = i * ts + lax.broadcasted_iota(jnp.int32, (ts, 1), 0)
    for gi, win in enumerate(POOL_WINDOWS):
        sl = slice(gi * POOL_GROUP_DIM, (gi + 1) * POOL_GROUP_DIM)
        u = u_ref[:, sl]
        halo = jnp.where(i > 0, halo_ref[:, sl], 0.0)
        ext = jnp.concatenate([halo, u], axis=0)
        shift = 1
        while shift < win:
            ext = ext + pltpu.roll(ext, shift, 0)
            shift *= 2
        count = jnp.minimum(pos + 1, win).astype(F32)
        pooled = ext[POOL_HALO:, :] / count - u
        y = _dot(pooled.astype(BF16), w_ref[gi]) + b_ref[:, sl]
        o_ref[:, sl] = (y * scale_ref[:, sl]).astype(o_ref.dtype)


def _multiscale_pool(u3, w_pool, b_pool, scale, ts=256):
    bsz, seq, dim = u3.shape
    assert seq % ts == 0 and ts % POOL_HALO == 0 and dim == POOL_DIM
    r = ts // POOL_HALO
    const = lambda bi, i: (0, 0)
    return pl.pallas_call(
        _pool_kernel,
        grid=(bsz, seq // ts),
        in_specs=[
            pl.BlockSpec((None, ts, dim), lambda bi, i: (bi, i, 0)),
            pl.BlockSpec((None, POOL_HALO, dim), lambda bi, i: (bi, jnp.maximum(i * r - 1, 0), 0)),
            pl.BlockSpec(w_pool.shape, lambda bi, i: (0, 0, 0)),
            pl.BlockSpec((1, dim), const),
            pl.BlockSpec((1, dim), const),
        ],
        out_specs=pl.BlockSpec((None, ts, dim), lambda bi, i: (bi, i, 0)),
        out_shape=jax.ShapeDtypeStruct((bsz, seq, dim), BF16),
        compiler_params=_cparams(("parallel", "parallel")),
        name="multiscale_pool",
    )(u3, u3, w_pool.astype(BF16), b_pool.reshape(1, -1), scale.reshape(1, -1))


def _even_mixer(x2, bsz, seq, w_in, w_dw, b_dw, conv_g, conv_b):
    u, q, k, v = _proj_in(x2, w_in, (2 * CONV_DIM, SB_DIM, SB_DIM, SB_DIM), (F32, BF16, BF16, BF16))
    y_conv = _conformer_conv(u.reshape(bsz, seq, -1), w_dw, b_dw, conv_g, conv_b)
    y_att = _sb_attention(q.reshape(bsz, seq, -1), k.reshape(bsz, seq, -1), v.reshape(bsz, seq, -1))
    return y_conv.reshape(bsz * seq, -1), y_att.reshape(bsz * seq, -1)


def _odd_mixer(x2, bsz, seq, w_in, mu, w0, w2, a0, a2, g2, k_k, k_a, r_k, lnx_g, lnx_b,
               w_pool, b_pool, pool_scale):
    p, u_pool = _proj_in(x2, w_in, (RW_IN_DIM, POOL_DIM), (F32, F32))
    y_rw = _rwkv_time_mix(p.reshape(bsz, seq, -1), mu, w0, w2, a0, a2, g2, k_k, k_a, r_k, lnx_g, lnx_b)
    y_pool = _multiscale_pool(u_pool.reshape(bsz, seq, -1), w_pool, b_pool, pool_scale)
    flat = lambda z: z.reshape(bsz * seq, -1)
    return flat(y_rw), flat(y_pool)


def kernel(x, ffn_in, ffn_out, ln_g, ln_b, e_w_in, e_w_dw, e_b_dw, e_conv_g, e_conv_b, e_w_out, o_w_in, o_mu, o_w0, o_w2, o_a0, o_a2, o_g2, o_k_k, o_k_a, o_r_k, o_lnx_g, o_lnx_b, o_w_pool, o_b_pool, o_pool_scale, o_w_out):
    bsz, seq, d = x.shape
    x2 = x.reshape(bsz * seq, d)
    depth = ffn_in.shape[0]
    w_in_all = ffn_in.astype(BF16)
    w_out_all = ffn_out.astype(BF16)
    for layer in range(depth):
        x2 = _ffn_ln(x2, w_in_all, w_out_all, (layer, 0), ln_g[layer, 0], ln_b[layer, 0])
        i = layer // 2
        if layer % 2 == 0:
            ya, yb = _even_mixer(x2, bsz, seq, e_w_in[i], e_w_dw[i], e_b_dw[i], e_conv_g[i], e_conv_b[i])
            w_out = e_w_out[i]
        else:
            ya, yb = _odd_mixer(x2, bsz, seq, o_w_in[i], o_mu[i], o_w0[i], o_w2[i], o_a0[i], o_a2[i],
                                o_g2[i], o_k_k[i], o_k_a[i], o_r_k[i], o_lnx_g[i], o_lnx_b[i],
                                o_w_pool[i], o_b_pool[i], o_pool_scale[i])
            w_out = o_w_out[i]
        x2 = _mix_ffn_ln(ya, yb, w_out, x2, ln_g[layer, 1], ln_b[layer, 1],
                         w_in_all, w_out_all, (layer, 1), ln_g[layer, 2], ln_b[layer, 2])
    return x2.reshape(bsz, seq, d)
```

```python
import functools

import jax
import jax.numpy as jnp
from jax import lax
from jax.experimental import pallas as pl
from jax.experimental.pallas import tpu as pltpu

F32 = jnp.float32
BF16 = jnp.bfloat16

D_MODEL = 1024
DEPTH = 2
DN_ALPHA = (2.0 * DEPTH) ** 0.25
LN_EPS = 1e-5
D_FF = 2816
FFN_RES = 0.5
CONV_DIM = 512
CONV_WIDTH = 31
HEAD_DIM = 64
SB_DIM = 512
RW_DIM = 512
DECAY_LORA = 64
ICLR_LORA = 64
GATE_LORA = 128
RW_IN_DIM = 3 * RW_DIM + DECAY_LORA + ICLR_LORA + GATE_LORA
GN_EPS = HEAD_DIM * 1e-5
POOL_WINDOWS = (2, 4, 8, 16)
POOL_GROUP_DIM = 128
POOL_DIM = 512

LANES = 128
SUBLANES = 8
VMEM_LIMIT = 56 * 1024 * 1024

FFN_CHUNK = 256
RW_CHUNK = 64
RW_PASSES = 1
INV_BLOCK = 16
ATT_BLOCK = 128
ATT_DEAD = -110.0


def _cparams(sem):
    return pltpu.CompilerParams(dimension_semantics=sem, vmem_limit_bytes=VMEM_LIMIT)


def _dot(a, b, dims="nn"):
    contract = {"nn": ((1,), (0,)), "nt": ((1,), (1,)), "tn": ((0,), (0,))}[dims]
    return lax.dot_general(a, b, (contract, ((), ())), preferred_element_type=F32)


def _split2(x):
    hi = x.astype(BF16)
    lo = (x - hi.astype(F32)).astype(BF16)
    return hi, lo


def _split3(x):
    hi = x.astype(BF16)
    r = x - hi.astype(F32)
    mid = r.astype(BF16)
    lo = (r - mid.astype(F32)).astype(BF16)
    return hi, mid, lo


def _dot_exact_lhs(a_bf16, b):
    bh, bm, bl = _split3(b)
    return _dot(a_bf16, bh) + (_dot(a_bf16, bm) + _dot(a_bf16, bl))


def _head_sums(x, hsum):
    hi, lo = _split2(x)
    return _dot(hi, hsum) + _dot(lo, hsum)


def _parts(x, passes):
    return (x.astype(BF16),) if passes == 1 else _split2(x)


def _dot_parts(a, b, dims="nn"):
    out = _dot(a[0], b[0], dims)
    if len(a) > 1 and len(b) > 1:
        return out + (_dot(a[0], b[1], dims) + _dot(a[1], b[0], dims))
    if len(b) > 1:
        return out + _dot(a[0], b[1], dims)
    if len(a) > 1:
        return out + _dot(a[1], b[0], dims)
    return out


def _map_parts(fn, *parts):
    return tuple(fn(*p) for p in zip(*parts))


def _softplus(x):
    return jnp.maximum(x, 0.0) + jnp.log(1.0 + jnp.exp(-jnp.abs(x)))


def _layer_norm_rows(y, g, b, eps):
    mu = jnp.mean(y, axis=-1, keepdims=True)
    yc = y - mu
    var = jnp.mean(yc * yc, axis=-1, keepdims=True)
    return yc * lax.rsqrt(var + eps) * g + b


def _ffn_ln_math(x, win_ref, wout_ref, g, b, acc_ref):
    xb = x.astype(BF16)
    for c in range(D_FF // FFN_CHUNK):
        lo = c * FFN_CHUNK
        gate = _dot(xb, win_ref[:, lo:lo + FFN_CHUNK])
        up = _dot(xb, win_ref[:, D_FF + lo:D_FF + lo + FFN_CHUNK])
        h = (gate * jax.nn.sigmoid(gate) * up).astype(BF16)
        part = _dot(h, wout_ref[lo:lo + FFN_CHUNK, :])
        if c == 0:
            acc_ref[...] = part
        else:
            acc_ref[...] += part
    y = DN_ALPHA * x + FFN_RES * acc_ref[...]
    return _layer_norm_rows(y, g, b, LN_EPS)


def _ffn_ln_kernel(x_ref, win_ref, wout_ref, g_ref, b_ref, o_ref, acc_ref):
    o_ref[...] = _ffn_ln_math(x_ref[...], win_ref, wout_ref, g_ref[...], b_ref[...], acc_ref)


def _mix_ffn_ln_kernel(ya_ref, yb_ref, wa_ref, wb_ref, x_ref, g1_ref, b1_ref,
                       win_ref, wout_ref, g2_ref, b2_ref, o_ref, acc_ref):
    m = _dot(ya_ref[...], wa_ref[...]) + _dot(yb_ref[...], wb_ref[...])
    x = _layer_norm_rows(DN_ALPHA * x_ref[...] + m, g1_ref[...], b1_ref[...], LN_EPS)
    o_ref[...] = _ffn_ln_math(x, win_ref, wout_ref, g2_ref[...], b2_ref[...], acc_ref)


def _ffn_specs(d, which):
    const = lambda i: (0, 0)
    pick = lambda i: (which[0], which[1], 0, 0)
    return [pl.BlockSpec((None, None, d, 2 * D_FF), pick, pipeline_mode=pl.Buffered(1)),
            pl.BlockSpec((None, None, D_FF, d), pick, pipeline_mode=pl.Buffered(1)),
            pl.BlockSpec((1, d), const),
            pl.BlockSpec((1, d), const)]


def _ffn_ln(x2, w_in_all, w_out_all, which, g, b, tm=512):
    t, d = x2.shape
    assert t % tm == 0 and D_FF % FFN_CHUNK == 0
    return pl.pallas_call(
        _ffn_ln_kernel,
        grid=(t // tm,),
        in_specs=[pl.BlockSpec((tm, d), lambda i: (i, 0))] + _ffn_specs(d, which),
        out_specs=pl.BlockSpec((tm, d), lambda i: (i, 0)),
        out_shape=jax.ShapeDtypeStruct((t, d), F32),
        scratch_shapes=[pltpu.VMEM((tm, d), F32)],
        compiler_params=_cparams(("parallel",)),
        name="ffn_ln",
    )(x2, w_in_all, w_out_all, g.reshape(1, d), b.reshape(1, d))


def _mix_ffn_ln(ya, yb, w_mix, x2, g1, b1, w_in_all, w_out_all, which, g2, b2, tm=512):
    t, d = x2.shape
    ka, kb = ya.shape[1], yb.shape[1]
    assert t % tm == 0 and ka == kb and w_mix.shape == (ka + kb, d)
    wm = w_mix.astype(BF16)
    const = lambda i: (0, 0)
    row = lambda i: (i, 0)
    vec = lambda z: z.reshape(1, d)
    return pl.pallas_call(
        _mix_ffn_ln_kernel,
        grid=(t // tm,),
        in_specs=[
            pl.BlockSpec((tm, ka), row),
            pl.BlockSpec((tm, kb), row),
            pl.BlockSpec((ka, d), const),
            pl.BlockSpec((kb, d), lambda i: (1, 0)),
            pl.BlockSpec((tm, d), row),
            pl.BlockSpec((1, d), const),
            pl.BlockSpec((1, d), const),
        ] + _ffn_specs(d, which),
        out_specs=pl.BlockSpec((tm, d), row),
        out_shape=jax.ShapeDtypeStruct((t, d), F32),
        scratch_shapes=[pltpu.VMEM((tm, d), F32)],
        compiler_params=_cparams(("parallel",)),
        name="mix_ffn_ln",
    )(ya, yb, wm, wm, x2, vec(g1), vec(b1), w_in_all, w_out_all, vec(g2), vec(b2))


def _proj_in_kernel(splits, x_ref, w_ref, *o_refs):
    xb = x_ref[...].astype(BF16)
    for (lo, hi), o_ref in zip(splits, o_refs):
        o_ref[...] = _dot(xb, w_ref[:, lo:hi]).astype(o_ref.dtype)


def _proj_in(x2, w, widths, dtypes, tm=512):
    t, d = x2.shape
    n = w.shape[1]
    assert t % tm == 0 and sum(widths) == n
    splits, lo = [], 0
    for wd in widths:
        splits.append((lo, lo + wd))
        lo += wd
    return pl.pallas_call(
        functools.partial(_proj_in_kernel, tuple(splits)),
        grid=(t // tm,),
        in_specs=[
            pl.BlockSpec((tm, d), lambda i: (i, 0)),
            pl.BlockSpec((d, n), lambda i: (0, 0), pipeline_mode=pl.Buffered(1)),
        ],
        out_specs=[pl.BlockSpec((tm, wd), lambda i: (i, 0)) for wd in widths],
        out_shape=[jax.ShapeDtypeStruct((t, wd), dt) for wd, dt in zip(widths, dtypes)],
        compiler_params=_cparams(("parallel",)),
        name="proj_in",
    )(x2, w.astype(BF16))


CONV_HALO = 32
CONV_ROWS = 128


def _glu(u):
    return u[:, :CONV_DIM] * jax.nn.sigmoid(u[:, CONV_DIM:])


def _conv_kernel(u_ref, halo_ref, w_ref, bdw_ref, g_ref, b_ref, o_ref, hs_ref, acc_ref):
    ts = u_ref.shape[0]
    i = pl.program_id(1)
    halo = _glu(halo_ref[...])
    hs_ref[0, 0:CONV_HALO, :] = jnp.where(i > 0, halo, 0.0)
    hs_ref[0, CONV_HALO:, :] = _glu(u_ref[...])
    span = ts + CONV_HALO - SUBLANES
    for r in range(1, SUBLANES):
        hs_ref[r, 0:span, :] = hs_ref[0, r:r + span, :]
    base = CONV_HALO - (CONV_WIDTH - 1)
    bias = jnp.broadcast_to(bdw_ref[...], (CONV_ROWS, CONV_DIM))

    def row_block(rb, carry):
        row0 = pl.multiple_of(rb * CONV_ROWS, CONV_ROWS)
        acc = bias
        for k in range(CONV_WIDTH):
            off = base + k
            window = hs_ref[off % SUBLANES, pl.ds(row0 + (off // SUBLANES) * SUBLANES, CONV_ROWS), :]
            acc = acc + w_ref[k:k + 1, :] * window
        acc_ref[pl.ds(row0, CONV_ROWS), :] = acc
        return carry

    lax.fori_loop(0, ts // CONV_ROWS, row_block, 0)
    y = _layer_norm_rows(acc_ref[...], g_ref[...], b_ref[...], LN_EPS)
    o_ref[...] = (y * jax.nn.sigmoid(y)).astype(o_ref.dtype)


def _conformer_conv(u3, w_dw, b_dw, g, b, ts=256):
    bsz, seq, two_c = u3.shape
    assert seq % ts == 0 and ts % CONV_HALO == 0 and two_c == 2 * CONV_DIM
    r = ts // CONV_HALO
    const = lambda bi, i: (0, 0)
    return pl.pallas_call(
        _conv_kernel,
        grid=(bsz, seq // ts),
        in_specs=[
            pl.BlockSpec((None, ts, two_c), lambda bi, i: (bi, i, 0)),
            pl.BlockSpec((None, CONV_HALO, two_c), lambda bi, i: (bi, jnp.maximum(i * r - 1, 0), 0)),
            pl.BlockSpec((CONV_WIDTH, CONV_DIM), const),
            pl.BlockSpec((1, CONV_DIM), const),
            pl.BlockSpec((1, CONV_DIM), const),
            pl.BlockSpec((1, CONV_DIM), const),
        ],
        out_specs=pl.BlockSpec((None, ts, CONV_DIM), lambda bi, i: (bi, i, 0)),
        out_shape=jax.ShapeDtypeStruct((bsz, seq, CONV_DIM), BF16),
        scratch_shapes=[pltpu.VMEM((SUBLANES, CONV_HALO + ts, CONV_DIM), F32), pltpu.VMEM((ts, CONV_DIM), F32)],
        compiler_params=_cparams(("parallel", "parallel")),
        name="conformer_conv",
    )(u3, u3, w_dw, b_dw.reshape(1, -1), g.reshape(1, -1), b.reshape(1, -1))


def _sb_key_block(qs, kbs, vbs, carries, u_aug):
    tk = kbs[0].shape[0]
    pairs = range(len(qs))
    z = [_dot(qs[p], kbs[p], "nt") for p in pairs]
    log_keep = [-_softplus(z[p]) for p in pairs]
    r = [_dot_parts(_split2(lk), (u_aug,)) for lk in log_keep]
    att = [jnp.exp(z[p] + carries[p] + r[p][:, :tk]) for p in pairs]
    pv = [_dot(att[p].astype(BF16), vbs[p]) for p in pairs]
    return pv, [carries[p] + r[p][:, tk:] for p in pairs]


def _sb_first_blocks(qs, kb0, vb0, kb1, vb1, u_aug, mask0, has_prev):
    tk = kb0[0].shape[0]
    pairs = range(len(qs))
    z0 = [_dot(qs[p], kb0[p], "nt") for p in pairs]
    z1 = [_dot(qs[p], kb1[p], "nt") for p in pairs]
    lk0 = [jnp.where(mask0, -_softplus(z0[p]), 0.0) for p in pairs]
    lk1 = [jnp.where(has_prev, -_softplus(z1[p]), 0.0) for p in pairs]
    r0 = [_dot_parts(_split2(lk), (u_aug,)) for lk in lk0]
    r1 = [_dot_parts(_split2(lk), (u_aug,)) for lk in lk1]
    att0 = [jnp.where(mask0, jnp.exp(z0[p] + r0[p][:, :tk]), 0.0) for p in pairs]
    att1 = [jnp.where(has_prev, jnp.exp(z1[p] + r0[p][:, tk:] + r1[p][:, :tk]), 0.0) for p in pairs]
    pv = [_dot(att0[p].astype(BF16), vb0[p]) + _dot(att1[p].astype(BF16), vb1[p]) for p in pairs]
    return pv, [r0[p][:, tk:] + r1[p][:, tk:] for p in pairs]


def _sb_attn_kernel(q_ref, k_ref, v_ref, o_ref, acc_ref, carry_ref):
    tq = q_ref.shape[0]
    tk = tq
    n_pairs = q_ref.shape[1] // LANES
    i = pl.program_id(2)
    head0 = lax.broadcasted_iota(jnp.int32, (tq, LANES), 1) < HEAD_DIM
    qs = []
    for p in range(n_pairs):
        q = (q_ref[:, p * LANES:(p + 1) * LANES].astype(F32) * (HEAD_DIM ** -0.5)).astype(BF16)
        qs.append(_stack_heads(q, head0))
    row = lax.broadcasted_iota(jnp.int32, (tk, 2 * tk), 0)
    col = lax.broadcasted_iota(jnp.int32, (tk, 2 * tk), 1)
    u_aug = jnp.where((row >= col) | (col >= tk), 1.0, 0.0).astype(BF16)
    qrow = lax.broadcasted_iota(jnp.int32, (2 * tq, tk), 0) % tq
    kcol = lax.broadcasted_iota(jnp.int32, (2 * tq, tk), 1)

    def tiles(ref, j):
        start = pl.multiple_of(j * tk, tk)
        return [ref[pl.ds(start, tk), p * LANES:(p + 1) * LANES] for p in range(n_pairs)]

    def key_block(j, first):
        if first:
            prev = jnp.maximum(j - 1, 0)
            pv, carries = _sb_first_blocks(qs, tiles(k_ref, j), tiles(v_ref, j), tiles(k_ref, prev),
                                           tiles(v_ref, prev), u_aug, kcol < qrow, j > 0)
        else:
            carries = [carry_ref[p] for p in range(n_pairs)]
            pv, carries = _sb_key_block(qs, tiles(k_ref, j), tiles(v_ref, j), carries, u_aug)
        live = None
        for p in range(n_pairs):
            if first:
                acc_ref[p] = pv[p]
            else:
                acc_ref[p] += pv[p]
            carry_ref[p] = carries[p]
            live = carries[p] if live is None else jnp.maximum(live, carries[p])
        return jnp.max(live)

    live = key_block(i, True)

    def cond(state):
        j, live = state
        return jnp.logical_and(j >= 0, live > ATT_DEAD)

    def body(state):
        j, _ = state
        return j - 1, key_block(j, False)

    lax.while_loop(cond, body, (i - 2, live))
    for p in range(n_pairs):
        o_ref[:, p * LANES:(p + 1) * LANES] = jnp.where(
            head0, acc_ref[p, :tq, :], acc_ref[p, tq:, :]).astype(o_ref.dtype)


def _sb_attention(q3, k3, v3, pairs=4):
    bsz, seq, dim = q3.shape
    tq = ATT_BLOCK
    width = pairs * LANES
    assert seq % tq == 0 and dim % width == 0
    qmap = lambda bi, hp, i: (bi, i, hp)
    kvmap = lambda bi, hp, i: (bi, 0, hp)
    return pl.pallas_call(
        _sb_attn_kernel,
        grid=(bsz, dim // width, seq // tq),
        in_specs=[
            pl.BlockSpec((None, tq, width), qmap),
            pl.BlockSpec((None, seq, width), kvmap),
            pl.BlockSpec((None, seq, width), kvmap),
        ],
        out_specs=pl.BlockSpec((None, tq, width), qmap),
        out_shape=jax.ShapeDtypeStruct((bsz, seq, dim), BF16),
        scratch_shapes=[pltpu.VMEM((pairs, 2 * tq, LANES), F32), pltpu.VMEM((pairs, 2 * tq, LANES), F32)],
        compiler_params=_cparams(("parallel", "parallel", "arbitrary")),
        name="sb_attention",
    )(q3, k3, v3)


RW_HALO = 8


def _rwkv_pre_math(p, prev_row, mu, w0, w2, a0, a2, g2, k_k, k_a, hsum):
    rowid = lax.broadcasted_iota(jnp.int32, p.shape, 0)
    p_prev = jnp.where(rowid == 0, prev_row, pltpu.roll(p, 1, 0))
    ps = p + (p_prev - p) * mu
    r = ps[:, 0:RW_DIM]
    k = ps[:, RW_DIM:2 * RW_DIM]
    v = ps[:, 2 * RW_DIM:3 * RW_DIM]
    lora = ps[:, 3 * RW_DIM:3 * RW_DIM + LANES]
    g_lr = ps[:, 3 * RW_DIM + LANES:]
    log_w = -_softplus(-(w0 + _dot(jnp.tanh(lora).astype(BF16), w2))) - 0.5
    lw = -jnp.exp(log_w)
    iclr = jax.nn.sigmoid(a0 + _dot(lora.astype(BF16), a2))
    gate = _dot(jax.nn.sigmoid(g_lr).astype(BF16), g2)
    kk = k * k_k
    sumsq = _head_sums(kk * kk, hsum)
    kk = kk * lax.rsqrt(jnp.maximum(sumsq, 1e-24))
    return r, lw, k * (1.0 + (iclr - 1.0) * k_a), v, -kk, kk * iclr, gate


def _stack_heads(x, head0):
    zero = jnp.zeros_like(x)
    return jnp.concatenate([jnp.where(head0, x, zero), jnp.where(head0, zero, x)], axis=0)


def _fold_heads(x):
    n = x.shape[0] // 2
    return x[:n] + x[n:]


def _unit_lower_inverses(n_mats, same_block, eye, parts):
    nd = [jnp.where(same_block, n, 0.0) for n in n_mats]
    rest_p = [parts(n - d) for n, d in zip(n_mats, nd)]
    dinv = [eye + d for d in nd]
    p_p = [parts(d) for d in nd]
    for _ in range(3):
        p_p = [parts(_dot_parts(pp, pp)) for pp in p_p]
        dinv = [d + _dot_parts(parts(d), pp) for d, pp in zip(dinv, p_p)]
    dinv_p = [parts(d) for d in dinv]
    f = [_dot_parts(dp, rp) for dp, rp in zip(dinv_p, rest_p)]
    f_p = [parts(x) for x in f]
    f2_p = [parts(_dot_parts(fp, fp)) for fp in f_p]
    g = [eye + x for x in f]
    g = [gi + _dot_parts(parts(gi), f2) for gi, f2 in zip(g, f2_p)]
    return [_dot_parts(parts(gi), dp) for gi, dp in zip(g, dinv_p)]


def _rwkv_masks():
    c = RW_CHUNK
    qr = jnp.arange(4 * c)[:, None]
    qc = jnp.arange(4 * c)[None, :]
    same_head = ((qr // c) % 2) == ((qc // c) % 2)
    t_idx, s_idx = qr % c, qc % c
    keep = same_head & ((s_idx < t_idx) | ((qr >= 2 * c) & (s_idx == t_idx)))
    pr = jnp.arange(2 * c)[:, None]
    pc = jnp.arange(2 * c)[None, :]
    same_block = (pr // INV_BLOCK) == (pc // INV_BLOCK)
    pair_diag = (pr // c) == (pc // c)
    eye = pr == pc
    right = jnp.concatenate([jnp.concatenate([same_block, pair_diag], axis=1),
                             jnp.concatenate([eye, jnp.zeros_like(eye)], axis=1)], axis=0)
    return jnp.concatenate([keep, right], axis=1).astype(F32)


def _unpack_masks(masks):
    n = 2 * RW_CHUNK
    return dict(keep=masks[:, :2 * n] > 0.5, same_block=masks[:n, 2 * n:3 * n] > 0.5,
                pair_diag=masks[:n, 3 * n:] > 0.5, eye=masks[n:, 2 * n:3 * n])


def _rwkv_chunk_prepare(r_ref, lw_ref, k_ref, v_ref, a_ref, b_ref, masks):
    c = RW_CHUNK
    n_chunks = lw_ref.shape[0] // c
    n_pairs = RW_DIM // LANES
    parts = functools.partial(_parts, passes=RW_PASSES)
    cat = lambda *xs: jnp.concatenate(xs, axis=0)
    tr = lax.broadcasted_iota(jnp.int32, (c, c), 0)
    tc = lax.broadcasted_iota(jnp.int32, (c, c), 1)
    lower = jnp.where(tr >= tc, 1.0, 0.0).astype(BF16)
    head0 = lax.broadcasted_iota(jnp.int32, (c, LANES), 1) < HEAD_DIM
    stack = functools.partial(_stack_heads, head0=head0)
    keep, same_block, eye, pair_diag = masks["keep"], masks["same_block"], masks["eye"], masks["pair_diag"]

    chains = []
    for ci in range(n_chunks):
        rows = slice(ci * c, (ci + 1) * c)
        lw = lw_ref[rows, :]
        cum = _dot_exact_lhs(lower, lw)
        last = cum[c - 1:c, :]
        a_t = parts(a_ref[rows, :] * jnp.exp(cum - lw))
        r_f32 = r_ref[rows, :] * jnp.exp(cum)
        r_t = parts(r_f32)
        g_inv = jnp.exp(-cum)
        g_rem = jnp.exp(last - cum)
        b_t = parts(b_ref[rows, :] * g_inv)
        k_t = parts(k_ref[rows, :] * g_inv)
        b_h = parts(b_ref[rows, :] * g_rem)
        k_h = parts(k_ref[rows, :] * g_rem)
        v_s = parts(v_ref[rows, :])
        g_last = jnp.exp(last)
        for m in range(n_pairs):
            sl = slice(m * LANES, (m + 1) * LANES)
            pick = lambda xs: tuple(x[:, sl] for x in xs)
            chains.append(dict(rows=rows, sl=sl, a=pick(a_t), r=pick(r_t), b=pick(b_t), k=pick(k_t),
                               bh=pick(b_h), kh=pick(k_h), v=pick(v_s), g_last=g_last[:, sl],
                               r_f32=r_f32[:, sl]))
    for ch in chains:
        lhs = _map_parts(lambda a, r: cat(stack(a), stack(r)), ch["a"], ch["r"])
        rhs = _map_parts(lambda b, k: cat(b, b, k, k), ch["b"], ch["k"])
        ch["q"] = jnp.where(keep, _dot_parts(lhs, rhs, "nt"), 0.0)
    t_invs = _unit_lower_inverses([ch["q"][:2 * c, :2 * c] for ch in chains], same_block, eye, parts)
    for ch, t_inv in zip(chains, t_invs):
        q = ch["q"]
        ch["t_inv"] = parts(_fold_heads(t_inv))
        ch["a_rb"] = parts(_fold_heads(q[2 * c:, :2 * c]))
        ch["a_r"] = parts(jnp.concatenate([_fold_heads(q[2 * c:, :2 * c]), _fold_heads(q[2 * c:, 2 * c:])], axis=1))
        ch["v_st"] = _map_parts(stack, ch["v"])
        ch["w"] = _dot_parts(ch["t_inv"], _map_parts(stack, ch["a"]))
        ch["rhs0"] = _dot_parts(parts(_fold_heads(q[:2 * c, 2 * c:])), ch["v_st"])
    for ch in chains:
        ch["x_p"] = parts(_dot_parts(ch["t_inv"], parts(stack(ch["rhs0"]))))
        ch["w_p"] = parts(ch["w"])
    for ch in chains:
        r_eff = ch["r_f32"] + _dot_parts(ch["a_rb"], _map_parts(stack, ch["w_p"]))
        ch["r_eff"] = parts(r_eff)
        ch["y0"] = _dot_parts(ch["a_r"], _map_parts(lambda x_, v_: cat(stack(x_), v_), ch["x_p"], ch["v_st"]))
        ch["m"] = parts(jnp.where(pair_diag, _dot_parts(ch["w_p"], ch["bh"], "tn"), 0.0))
        ch["c"] = jnp.where(pair_diag, _dot_parts(_map_parts(cat, ch["x_p"], ch["v"]),
                                                  _map_parts(cat, ch["bh"], ch["kh"]), "tn"), 0.0)
    return chains


def _rwkv_chunk_chain(chains, y_ref, s_ref):
    n_pairs = RW_DIM // LANES
    n_chunks = len(chains) // n_pairs
    parts = functools.partial(_parts, passes=RW_PASSES)
    state = [s_ref[m] for m in range(n_pairs)]
    for ci in range(n_chunks):
        group = chains[ci * n_pairs:(ci + 1) * n_pairs]
        s_p = [parts(s) for s in state]
        for m, ch in enumerate(group):
            y_ref[ch["rows"], ch["sl"]] = _dot_parts(ch["r_eff"], s_p[m], "nt") + ch["y0"]
            state[m] = state[m] * ch["g_last"] + (_dot_parts(s_p[m], ch["m"]) + ch["c"])
    for m in range(n_pairs):
        s_ref[m] = state[m]


def _rwkv_post_math(y, r, k, v, gate, r_k, lnx_g, lnx_b, hsum):
    inv_n = 1.0 / HEAD_DIM
    mu = _head_sums(y, hsum) * inv_n
    yc = y - mu
    var = _head_sums(yc * yc, hsum) * inv_n
    yn = yc * lax.rsqrt(var + GN_EPS) * lnx_g + lnx_b
    bonus = _head_sums(r * k * r_k, hsum) * v
    return (yn + bonus) * gate


def _rwkv_mix_kernel(p_ref, halo_ref, mu_ref, w0_ref, w2_ref, a0_ref, a2_ref, g2_ref, kk_ref, ka_ref,
                     rk_ref, lnxg_ref, lnxb_ref, hsum_ref, masks_ref, o_ref, y_ref, s_ref):
    @pl.when(pl.program_id(1) == 0)
    def _():
        s_ref[...] = jnp.zeros_like(s_ref)

    hsum = hsum_ref[...]
    masks = _unpack_masks(masks_ref[...])
    prev_row = jnp.where(pl.program_id(1) > 0, halo_ref[RW_HALO - 1:RW_HALO, :], 0.0)
    r, lw, k, v, a, b, gate = _rwkv_pre_math(
        p_ref[...], prev_row, mu_ref[...], w0_ref[...], w2_ref[...], a0_ref[...], a2_ref[...], g2_ref[...],
        kk_ref[...], ka_ref[...], hsum)
    _rwkv_chunk_chain(_rwkv_chunk_prepare(r, lw, k, v, a, b, masks), y_ref, s_ref)
    out = _rwkv_post_math(y_ref[...], r, k, v, gate, rk_ref[...], lnxg_ref[...], lnxb_ref[...], hsum)
    o_ref[...] = out.astype(o_ref.dtype)


def _rwkv_time_mix(p3, mu, w0, w2, a0, a2, g2, k_k, k_a, r_k, lnx_g, lnx_b, chunks=4):
    bsz, seq, width = p3.shape
    ts = RW_CHUNK * chunks
    assert seq % ts == 0 and ts % RW_HALO == 0 and width == RW_IN_DIM
    r = ts // RW_HALO
    head_of = jnp.arange(RW_DIM) // HEAD_DIM
    hsum = (head_of[:, None] == head_of[None, :]).astype(BF16)
    zeros = jnp.zeros((DECAY_LORA, RW_DIM), F32)
    w2p = jnp.concatenate([w2, zeros], axis=0).astype(BF16)
    a2p = jnp.concatenate([zeros, a2], axis=0).astype(BF16)
    g2 = g2.astype(BF16)
    const = lambda bi, i: (0, 0)
    vec = pl.BlockSpec((1, RW_DIM), const)
    mat = pl.BlockSpec((LANES, RW_DIM), const)
    flat = lambda z: z.reshape(1, -1)
    return pl.pallas_call(
        _rwkv_mix_kernel,
        grid=(bsz, seq // ts),
        in_specs=[
            pl.BlockSpec((None, ts, width), lambda bi, i: (bi, i, 0)),
            pl.BlockSpec((None, RW_HALO, width), lambda bi, i: (bi, jnp.maximum(i * r - 1, 0), 0)),
            pl.BlockSpec((1, width), const),
            vec, mat, vec, mat, mat, vec, vec, vec, vec, vec,
            pl.BlockSpec((RW_DIM, RW_DIM), const),
            pl.BlockSpec((4 * RW_CHUNK, 8 * RW_CHUNK), const),
        ],
        out_specs=pl.BlockSpec((None, ts, RW_DIM), lambda bi, i: (bi, i, 0)),
        out_shape=jax.ShapeDtypeStruct((bsz, seq, RW_DIM), BF16),
        scratch_shapes=[pltpu.VMEM((ts, RW_DIM), F32), pltpu.VMEM((RW_DIM // LANES, LANES, LANES), F32)],
        compiler_params=_cparams(("parallel", "arbitrary")),
        name="rwkv_time_mix",
    )(p3, p3, flat(mu), flat(w0), w2p, flat(a0), a2p, g2, flat(k_k), flat(k_a), flat(r_k), flat(lnx_g), flat(lnx_b),
      hsum, _rwkv_masks())


POOL_HALO = 16


def _pool_kernel(u_ref, halo_ref, w_ref, b_ref, scale_ref, o_ref):
    ts = u_ref.shape[0]
    i = pl.program_id(1)
    pos = i * ts + lax.broadcasted_iota(jnp.int32, (ts, 1), 0)
    for gi, win in enumerate(POOL_WINDOWS):
        sl = slice(gi * POOL_GROUP_DIM, (gi + 1) * POOL_GROUP_DIM)
        u = u_ref[:, sl]
        halo = jnp.where(i > 0, halo_ref[:, sl], 0.0)
        ext = jnp.concatenate([halo, u], axis=0)
        shift = 1
        while shift < win:
            ext = ext + pltpu.roll(ext, shift, 0)
            shift *= 2
        count = jnp.minimum(pos + 1, win).astype(F32)
        pooled = ext[POOL_HALO:, :] / count - u
        y = _dot(pooled.astype(BF16), w_ref[gi]) + b_ref[:, sl]
        o_ref[:, sl] = (y * scale_ref[:, sl]).astype(o_ref.dtype)


def _multiscale_pool(u3, w_pool, b_pool, scale, ts=256):
    bsz, seq, dim = u3.shape
    assert seq % ts == 0 and ts % POOL_HALO == 0 and dim == POOL_DIM
    r = ts // POOL_HALO
    const = lambda bi, i: (0, 0)
    return pl.pallas_call(
        _pool_kernel,
        grid=(bsz, seq // ts),
        in_specs=[
            pl.BlockSpec((None, ts, dim), lambda bi, i: (bi, i, 0)),
            pl.BlockSpec((None, POOL_HALO, dim), lambda bi, i: (bi, jnp.maximum(i * r - 1, 0), 0)),
            pl.BlockSpec(w_pool.shape, lambda bi, i: (0, 0, 0)),
            pl.BlockSpec((1, dim), const),
            pl.BlockSpec((1, dim), const),
        ],
        out_specs=pl.BlockSpec((None, ts, dim), lambda bi, i: (bi, i, 0)),
        out_shape=jax.ShapeDtypeStruct((bsz, seq, dim), BF16),
        compiler_params=_cparams(("parallel", "parallel")),
        name="multiscale_pool",
    )(u3, u3, w_pool.astype(BF16), b_pool.reshape(1, -1), scale.reshape(1, -1))


def _even_mixer(x2, bsz, seq, w_in, w_dw, b_dw, conv_g, conv_b):
    u, q, k, v = _proj_in(x2, w_in, (2 * CONV_DIM, SB_DIM, SB_DIM, SB_DIM), (F32, BF16, BF16, BF16))
    y_conv = _conformer_conv(u.reshape(bsz, seq, -1), w_dw, b_dw, conv_g, conv_b)
    y_att = _sb_attention(q.reshape(bsz, seq, -1), k.reshape(bsz, seq, -1), v.reshape(bsz, seq, -1))
    return y_conv.reshape(bsz * seq, -1), y_att.reshape(bsz * seq, -1)


def _odd_mixer(x2, bsz, seq, w_in, mu, w0, w2, a0, a2, g2, k_k, k_a, r_k, lnx_g, lnx_b,
               w_pool, b_pool, pool_scale):
    p, u_pool = _proj_in(x2, w_in, (RW_IN_DIM, POOL_DIM), (F32, F32))
    y_rw = _rwkv_time_mix(p.reshape(bsz, seq, -1), mu, w0, w2, a0, a2, g2, k_k, k_a, r_k, lnx_g, lnx_b)
    y_pool = _multiscale_pool(u_pool.reshape(bsz, seq, -1), w_pool, b_pool, pool_scale)
    flat = lambda z: z.reshape(bsz * seq, -1)
    return flat(y_rw), flat(y_pool)


def kernel(x, ffn_in, ffn_out, ln_g, ln_b, e_w_in, e_w_dw, e_b_dw, e_conv_g, e_conv_b, e_w_out, o_w_in, o_mu, o_w0, o_w2, o_a0, o_a2, o_g2, o_k_k, o_k_a, o_r_k, o_lnx_g, o_lnx_b, o_w_pool, o_b_pool, o_pool_scale, o_w_out):
    bsz, seq, d = x.shape
    x2 = x.reshape(bsz * seq, d)
    depth = ffn_in.shape[0]
    w_in_all = ffn_in.astype(BF16)
    w_out_all = ffn_out.astype(BF16)
    for layer in range(depth):
        x2 = _ffn_ln(x2, w_in_all, w_out_all, (layer, 0), ln_g[layer, 0], ln_b[layer, 0])
        i = layer // 2
        if layer % 2 == 0:
            ya, yb = _even_mixer(x2, bsz, seq, e_w_in[i], e_w_dw[i], e_b_dw[i], e_conv_g[i], e_conv_b[i])
            w_out = e_w_out[i]
        else:
            ya, yb = _odd_mixer(x2, bsz, seq, o_w_in[i], o_mu[i], o_w0[i], o_w2[i], o_a0[i], o_a2[i],
                                o_g2[i], o_k_k[i], o_k_a[i], o_r_k[i], o_lnx_g[i], o_lnx_b[i],
                                o_w_pool[i], o_b_pool[i], o_pool_scale[i])
            w_out = o_w_out[i]
        x2 = _mix_ffn_ln(ya, yb, w_out, x2, ln_g[layer, 1], ln_b[layer, 1],
                         w_in_all, w_out_all, (layer, 1), ln_g[layer, 2], ln_b[layer, 2])
    return x2.reshape(bsz, seq, d)
```
